```python
import math
import jax
import jax.numpy as jnp
from jax import lax
import numpy as np

D_MODEL = 1024
BATCH = 2
SEQ = 8192
DEPTH = 1

HEAD_DIM = 64
A_Q_HEADS = 8
A_KV_HEADS = 2
B_Q_HEADS = 8
B_KV_HEADS = 2
BRANCH_WIDTH = A_Q_HEADS * HEAD_DIM
D_FF = 4 * D_MODEL
GRID_W = 64
Q_BLOCK = 128
WINDOW = 128
BAND_BLOCK = WINDOW
ROPE_THETA = 10000.0
AXIAL_THETA = 10000.0
NORM_EPS = 1e-6
NEG_INF = -1e30

A_Q_W = A_Q_HEADS * HEAD_DIM
A_KV_W = A_KV_HEADS * HEAD_DIM
B_Q_W = B_Q_HEADS * HEAD_DIM
B_KV_W = B_KV_HEADS * HEAD_DIM
IN_SPLITS = [A_Q_W, A_KV_W, A_KV_W, B_Q_W, B_KV_W, B_KV_W, D_MODEL, D_MODEL]
IN_WIDTH = sum(IN_SPLITS)

kernel_name = "hybrid_gated_axial_window_attention_block"


def rms_norm(x, g):
    xf = x.astype(jnp.float32)
    y = xf * lax.rsqrt(jnp.mean(xf * xf, axis=-1, keepdims=True) + NORM_EPS)
    return (y * g.astype(jnp.float32)).astype(x.dtype)


def rope_cos_sin(pos, dim, theta):
    inv = theta ** (-jnp.arange(0, dim, 2, dtype=jnp.float32) / dim)
    ang = pos.astype(jnp.float32)[:, None] * inv[None, :]
    return jnp.cos(ang), jnp.sin(ang)


def apply_rope(x, cos, sin):
    xf = x.astype(jnp.float32)
    half = xf.shape[-1] // 2
    x1, x2 = xf[..., :half], xf[..., half:]
    c = cos[None, :, None, :]
    s = sin[None, :, None, :]
    return jnp.concatenate([x1 * c - x2 * s, x1 * s + x2 * c], axis=-1).astype(x.dtype)


def apply_axial_rope(x, row, col):
    half = x.shape[-1] // 2
    cr, sr = rope_cos_sin(row, half, AXIAL_THETA)
    cc, sc = rope_cos_sin(col, half, AXIAL_THETA)
    return jnp.concatenate([apply_rope(x[..., :half], cr, sr),
                            apply_rope(x[..., half:], cc, sc)], axis=-1)


def global_attention(q, k, v):
    b, s, hq, dh = q.shape
    hkv = k.shape[2]
    g = hq // hkv
    nb = s // Q_BLOCK
    scale = dh ** -0.5
    qb = q.reshape(b, nb, Q_BLOCK, hkv, g, dh).transpose(1, 0, 2, 3, 4, 5)

    def one_block(qblk):
        sc = jnp.einsum('bqkgd,bskd->bkgqs', qblk, k).astype(jnp.float32) * scale
        p = jax.nn.softmax(sc, axis=-1).astype(v.dtype)
        return jnp.einsum('bkgqs,bskd->bqkgd', p, v)

    o = lax.map(one_block, qb)
    return o.transpose(1, 0, 2, 3, 4, 5).reshape(b, s, hq * dh)


def window_sink_attention(q, k, v, sink):
    b, s, hq, dh = q.shape
    hkv = k.shape[2]
    g = hq // hkv
    nb = s // BAND_BLOCK
    scale = dh ** -0.5
    pad = ((0, 0), (BAND_BLOCK, BAND_BLOCK), (0, 0), (0, 0))
    kr = jnp.pad(k, pad).reshape(b, nb + 2, BAND_BLOCK, hkv, dh)
    vr = jnp.pad(v, pad).reshape(b, nb + 2, BAND_BLOCK, hkv, dh)
    kb = jnp.concatenate([kr[:, :-2], kr[:, 1:-1], kr[:, 2:]], axis=2)
    vb = jnp.concatenate([vr[:, :-2], vr[:, 1:-1], vr[:, 2:]], axis=2)
    qb = q.reshape(b, nb, BAND_BLOCK, hkv, g, dh)
    sc = jnp.einsum('bnqkgd,bnskd->bnkgqs', qb, kb).astype(jnp.float32) * scale
    blk = jnp.arange(nb, dtype=jnp.int32)[:, None] * BAND_BLOCK
    qpos = blk + jnp.arange(BAND_BLOCK, dtype=jnp.int32)[None, :]
    kpos = blk - BAND_BLOCK + jnp.arange(3 * BAND_BLOCK, dtype=jnp.int32)[None, :]
    valid = (jnp.abs(kpos[:, None, :] - qpos[:, :, None]) <= WINDOW) \
        & (kpos[:, None, :] >= 0) & (kpos[:, None, :] < s)
    sc = jnp.where(valid[None, :, None, None], sc, NEG_INF)
    sink_l = jnp.broadcast_to(sink.astype(jnp.float32).reshape(1, 1, hkv, g, 1, 1),
                              sc.shape[:-1] + (1,))
    p = jax.nn.softmax(jnp.concatenate([sc, sink_l], axis=-1), axis=-1)[..., :-1]
    o = jnp.einsum('bnkgqs,bnskd->bnqkgd', p.astype(v.dtype), vb)
    return o.reshape(b, s, hq * dh)


def setup_inputs(seed: int = 0) -> dict:
    key = jax.random.key(seed)
    ks = jax.random.split(key, 16)
    f32 = jnp.float32
    d = D_MODEL

    def nrm(k, shape, scale):
        return jax.random.normal(k, shape, f32) * scale

    return {
        "x": nrm(ks[0], (BATCH, SEQ, d), 1.0),
        "c": nrm(ks[1], (BATCH, d), 1.0),
        "w_ada": nrm(ks[2], (DEPTH, d, 6 * d), 0.02),
        "b_ada": nrm(ks[3], (DEPTH, 6 * d), 0.02),
        "norm1_g": 1.0 + nrm(ks[4], (DEPTH, d), 0.02),
        "w_in": nrm(ks[5], (DEPTH, d, IN_WIDTH), d ** -0.5),
        "q_norm_a": 1.0 + nrm(ks[6], (DEPTH, HEAD_DIM), 0.02),
        "k_norm_a": 1.0 + nrm(ks[7], (DEPTH, HEAD_DIM), 0.02),
        "sink_b": nrm(ks[8], (DEPTH, B_Q_HEADS), 0.5),
        "w_branch": nrm(ks[9], (DEPTH, 2, BRANCH_WIDTH, d), BRANCH_WIDTH ** -0.5),
        "w_out": nrm(ks[10], (DEPTH, d, d), d ** -0.5),
        "norm2_g": 1.0 + nrm(ks[11], (DEPTH, d), 0.02),
        "w_mlp_in": nrm(ks[12], (DEPTH, d, D_FF), d ** -0.5),
        "w_mlp_out": nrm(ks[13], (DEPTH, D_FF, d), D_FF ** -0.5),
        "final_g": 1.0 + nrm(ks[14], (d,), 0.02),
    }


def reference(x, c, w_ada, b_ada, norm1_g, w_in, q_norm_a, k_norm_a, sink_b,
              w_branch, w_out, norm2_g, w_mlp_in, w_mlp_out, final_g):
    b, s, d = x.shape
    rows = s // GRID_W
    t = jnp.arange(s, dtype=jnp.int32)
    row_ids = jnp.repeat(jnp.arange(rows, dtype=jnp.int32), GRID_W)
    col_ids = jnp.tile(jnp.arange(GRID_W, dtype=jnp.int32), rows)
    cos1, sin1 = rope_cos_sin(t, HEAD_DIM, ROPE_THETA)
    offsets = np.cumsum(IN_SPLITS)[:-1].tolist()

    for l in range(DEPTH):
        mod = jax.nn.silu(c) @ w_ada[l] + b_ada[l]
        shift1, scale1, gate1, shift2, scale2, gate2 = jnp.split(mod, 6, axis=-1)

        h = rms_norm(x, norm1_g[l]) * (1.0 + scale1[:, None]) + shift1[:, None]
        proj = h @ w_in[l]
        qa, ka, va, qb, kb, vb, ga, gb = jnp.split(proj, offsets, axis=-1)

        qa = rms_norm(qa.reshape(b, s, A_Q_HEADS, HEAD_DIM), q_norm_a[l])
        ka = rms_norm(ka.reshape(b, s, A_KV_HEADS, HEAD_DIM), k_norm_a[l])
        qa = apply_axial_rope(qa, row_ids, col_ids)
        ka = apply_axial_rope(ka, row_ids, col_ids)
        ya = global_attention(qa, ka, va.reshape(b, s, A_KV_HEADS, HEAD_DIM))

        qb = apply_rope(qb.reshape(b, s, B_Q_HEADS, HEAD_DIM), cos1, sin1)
        kb = apply_rope(kb.reshape(b, s, B_KV_HEADS, HEAD_DIM), cos1, sin1)
        yb = window_sink_attention(qb, kb, vb.reshape(b, s, B_KV_HEADS, HEAD_DIM), sink_b[l])

        ua = ya @ w_branch[l, 0]
        ub = yb @ w_branch[l, 1]
        merged = jax.nn.sigmoid(ga) * ua + jax.nn.sigmoid(gb) * ub
        x = x + gate1[:, None] * (merged @ w_out[l])

        h2 = rms_norm(x, norm2_g[l]) * (1.0 + scale2[:, None]) + shift2[:, None]
        hid = jnp.square(jax.nn.relu(h2 @ w_mlp_in[l]))
        x = x + gate2[:, None] * (hid @ w_mlp_out[l])

    return rms_norm(x, final_g)
```

```python
import functools
import math

import jax
import jax.numpy as jnp
import numpy as np
from jax import lax
from jax.experimental import pallas as pl
from jax.experimental.pallas import tpu as pltpu

F32 = jnp.float32
BF16 = jnp.bfloat16

HEAD_DIM = 64
Q_HEADS = 8
KV_HEADS = 2
GROUP = Q_HEADS // KV_HEADS
GRID_W = 64
WINDOW = 128
ROPE_THETA = 10000.0
AXIAL_THETA = 10000.0
NORM_EPS = 1e-6
NEG_INF = -1e30
LOG2E = math.log2(math.e)

V7X_BF16_SUBLANES = 16
V7X_VMEM_LIMIT = 56 * 1024 * 1024

V_ROWS = HEAD_DIM + V7X_BF16_SUBLANES


def _rms(x, eps=NORM_EPS):
    return x * lax.rsqrt(jnp.mean(x * x, axis=-1, keepdims=True) + eps)


def _mod_kernel(c_ref, w_ref, b_ref, o_ref):
    c = c_ref[...]
    a = c * (1.0 / (1.0 + jnp.exp(-c)))
    w = w_ref[...]
    a_hi = a.astype(BF16)
    a_lo = (a - a_hi.astype(F32)).astype(BF16)
    w_hi = w.astype(BF16)
    w_lo = (w - w_hi.astype(F32)).astype(BF16)
    dot = functools.partial(jnp.dot, preferred_element_type=F32)
    o_ref[...] = dot(a_hi, w_hi) + (dot(a_hi, w_lo) + dot(a_lo, w_hi)) + b_ref[...]


def _mod_call(c_pad, w_ada, b_ada, *, tn=1024):
    rows, d = c_pad.shape
    n = w_ada.shape[1]
    return pl.pallas_call(
        _mod_kernel,
        out_shape=jax.ShapeDtypeStruct((rows, n), F32),
        grid=(n // tn,),
        in_specs=[
            pl.BlockSpec((rows, d), lambda j: (0, 0)),
            pl.BlockSpec((d, tn), lambda j: (0, j)),
            pl.BlockSpec((1, tn), lambda j: (0, j)),
        ],
        out_specs=pl.BlockSpec((rows, tn), lambda j: (0, j)),
        compiler_params=pltpu.CompilerParams(
            dimension_semantics=("arbitrary",), vmem_limit_bytes=V7X_VMEM_LIMIT),
        name="adaln_mod",
    )(c_pad, w_ada, b_ada)


def _rope_t(x, cos, sin, half):
    parts = []
    for base in range(0, HEAD_DIM, 2 * half):
        parts.append(x[base + half:base + 2 * half])
        parts.append(x[base:base + half])
    x_rot = jnp.concatenate(parts, axis=0)
    return x * cos + x_rot * sin


def _proj_kernel(x_ref, mod_ref, g1_ref, wt_ref, wg_ref, gq_ref, gk_ref,
                 cos_a_ref, sin_a_ref, cos_b_ref, sin_b_ref,
                 qa_ref, ka_ref, va_ref, qb_ref, kb_ref, vb_ref, sg_ref):
    x = x_ref[0]
    shift1 = mod_ref[0, 0:1, :]
    scale1 = mod_ref[0, 1:2, :]
    h = (_rms(x) * g1_ref[...] * (1.0 + scale1) + shift1).astype(BF16)

    pt = lax.dot_general(wt_ref[...], h, (((1,), (1,)), ((), ())),
                         preferred_element_type=F32)
    tm = pt.shape[1]
    q_scale = HEAD_DIM ** -0.5 * LOG2E
    ones = jnp.ones((V_ROWS - HEAD_DIM, tm), BF16)

    cos_a, sin_a = cos_a_ref[...], sin_a_ref[...]
    cos_b, sin_b = cos_b_ref[...], sin_b_ref[...]
    gq, gk = gq_ref[...], gk_ref[...]

    def head(row):
        return pt[row:row + HEAD_DIM]

    def normed(t, g):
        ms = jnp.sum(t * t, axis=0, keepdims=True) * (1.0 / HEAD_DIM)
        return t * lax.rsqrt(ms + NORM_EPS) * g

    row = 0
    for hq in range(Q_HEADS):
        t = _rope_t(normed(head(row), gq), cos_a, sin_a, HEAD_DIM // 4)
        qa_ref[0, hq] = (t * q_scale).astype(BF16)
        row += HEAD_DIM
    for hk in range(KV_HEADS):
        t = _rope_t(normed(head(row), gk), cos_a, sin_a, HEAD_DIM // 4)
        ka_ref[0, hk] = t.T.astype(BF16)
        row += HEAD_DIM
    for hk in range(KV_HEADS):
        va_ref[0, hk] = jnp.concatenate([head(row).astype(BF16), ones], axis=0)
        row += HEAD_DIM
    for hq in range(Q_HEADS):
        t = _rope_t(head(row), cos_b, sin_b, HEAD_DIM // 2)
        qb_ref[0, hq] = (t * q_scale).astype(BF16)
        row += HEAD_DIM
    for hk in range(KV_HEADS):
        t = _rope_t(head(row), cos_b, sin_b, HEAD_DIM // 2)
        kb_ref[0, hk] = t.T.astype(BF16)
        row += HEAD_DIM
    for hk in range(KV_HEADS):
        vb_ref[0, hk] = jnp.concatenate([head(row).astype(BF16), ones], axis=0)
        row += HEAD_DIM

    g = jnp.dot(h, wg_ref[...], preferred_element_type=F32)
    sg_ref[0] = (1.0 / (1.0 + jnp.exp(-g))).astype(BF16)


def _proj_call(x, mod3, g1, wt, wg, gq, gk, cos_a, sin_a, cos_b, sin_b, *, tm=512):
    b, s, d = x.shape
    n_qkv = wt.shape[0]
    n_g = wg.shape[1]
    const2 = lambda bi, i: (0, 0)
    tab = pl.BlockSpec((HEAD_DIM, tm), lambda bi, i: (0, i))
    q_shape = jax.ShapeDtypeStruct((b, Q_HEADS, HEAD_DIM, s), BF16)
    k_shape = jax.ShapeDtypeStruct((b, KV_HEADS, s, HEAD_DIM), BF16)
    v_shape = jax.ShapeDtypeStruct((b, KV_HEADS, V_ROWS, s), BF16)
    q_spec = pl.BlockSpec((1, Q_HEADS, HEAD_DIM, tm), lambda bi, i: (bi, 0, 0, i))
    k_spec = pl.BlockSpec((1, KV_HEADS, tm, HEAD_DIM), lambda bi, i: (bi, 0, i, 0))
    v_spec = pl.BlockSpec((1, KV_HEADS, V_ROWS, tm), lambda bi, i: (bi, 0, 0, i))
    return pl.pallas_call(
        _proj_kernel,
        out_shape=(q_shape, k_shape, v_shape, q_shape, k_shape, v_shape,
                   jax.ShapeDtypeStruct((b, s, n_g), BF16)),
        grid=(b, s // tm),
        in_specs=[
            pl.BlockSpec((1, tm, d), lambda bi, i: (bi, i, 0)),
            pl.BlockSpec((1, 6, d), lambda bi, i: (bi, 0, 0)),
            pl.BlockSpec((1, d), const2),
            pl.BlockSpec((n_qkv, d), const2),
            pl.BlockSpec((d, n_g), const2),
            pl.BlockSpec((HEAD_DIM, 1), const2),
            pl.BlockSpec((HEAD_DIM, 1), const2),
            tab, tab, tab, tab,
        ],
        out_specs=(q_spec, k_spec, v_spec, q_spec, k_spec, v_spec,
                   pl.BlockSpec((1, tm, n_g), lambda bi, i: (bi, i, 0))),
        compiler_params=pltpu.CompilerParams(
            dimension_semantics=("arbitrary", "arbitrary"),
            vmem_limit_bytes=V7X_VMEM_LIMIT),
        name="in_proj",
    )(x, mod3, g1, wt, wg, gq, gk, cos_a, sin_a, cos_b, sin_b)


def _attn_a_kernel(q_ref, k_ref, v_ref, o_ref, *, tk):
    tq = q_ref.shape[3]
    s_len = k_ref.shape[2]
    outs = []
    for hq in range(GROUP):
        q = q_ref[0, hq]

        def body(c, carry, q=q):
            m, acc = carry
            start = pl.multiple_of(c * tk, tk)
            kc = k_ref[0, 0, pl.ds(start, tk), :]
            st = jnp.dot(kc, q, preferred_element_type=F32)
            m_new = jnp.maximum(m, jnp.max(st, axis=0, keepdims=True))
            alpha = jnp.exp2(m - m_new)
            p = jnp.exp2(st - m_new).astype(BF16)
            vc = v_ref[0, 0, :, pl.ds(start, tk)]
            acc = acc * alpha + jnp.dot(vc, p, preferred_element_type=F32)
            return m_new, acc

        m0 = jnp.full((1, tq), NEG_INF, F32)
        acc0 = jnp.zeros((V_ROWS, tq), F32)
        _, acc = lax.fori_loop(0, s_len // tk, body, (m0, acc0))
        o = acc[:HEAD_DIM] * (1.0 / acc[HEAD_DIM:HEAD_DIM + 1])
        outs.append(o.T)
    o_ref[0] = jnp.concatenate(outs, axis=1).astype(BF16)


def _attn_a_call(qt, k, vt, *, tq=512, tk=256):
    b, _, _, s = qt.shape
    return pl.pallas_call(
        functools.partial(_attn_a_kernel, tk=tk),
        out_shape=jax.ShapeDtypeStruct((b, s, Q_HEADS * HEAD_DIM), BF16),
        grid=(b, KV_HEADS, s // tq),
        in_specs=[
            pl.BlockSpec((1, GROUP, HEAD_DIM, tq), lambda bi, hk, i: (bi, hk, 0, i)),
            pl.BlockSpec((1, 1, s, HEAD_DIM), lambda bi, hk, i: (bi, hk, 0, 0)),
            pl.BlockSpec((1, 1, V_ROWS, s), lambda bi, hk, i: (bi, hk, 0, 0)),
        ],
        out_specs=pl.BlockSpec((1, tq, GROUP * HEAD_DIM), lambda bi, hk, i: (bi, i, hk)),
        compiler_params=pltpu.CompilerParams(
            dimension_semantics=("arbitrary", "arbitrary", "arbitrary"),
            vmem_limit_bytes=V7X_VMEM_LIMIT),
        name="attn_global",
    )(qt, k, vt)


def _attn_b_kernel(sink_ref, bias_ref, q_ref, kl_ref, kc_ref, kr_ref,
                   vl_ref, vc_ref, vr_ref, o_ref):
    hk = pl.program_id(1)
    w = q_ref.shape[3]
    q = jnp.concatenate([q_ref[0, hq] for hq in range(GROUP)], axis=1)
    k = jnp.concatenate([kl_ref[0, 0], kc_ref[0, 0], kr_ref[0, 0]], axis=0)
    st = jnp.dot(k, q, preferred_element_type=F32)
    bias = bias_ref[0]
    st = st + jnp.concatenate([bias] * GROUP, axis=1)
    sink = jnp.concatenate(
        [jnp.full((1, w), sink_ref[hk * GROUP + hq] * LOG2E, F32) for hq in range(GROUP)],
        axis=1)
    m = jnp.maximum(jnp.max(st, axis=0, keepdims=True), sink)
    p = jnp.exp2(st - m).astype(BF16)
    v = jnp.concatenate([vl_ref[0, 0], vc_ref[0, 0], vr_ref[0, 0]], axis=1)
    acc = jnp.dot(v, p, preferred_element_type=F32)
    denom = acc[HEAD_DIM:HEAD_DIM + 1] + jnp.exp2(sink - m)
    o = acc[:HEAD_DIM] * (1.0 / denom)
    o_ref[0] = jnp.concatenate(
        [o[:, hq * w:(hq + 1) * w].T for hq in range(GROUP)], axis=1).astype(BF16)


def _band_bias():
    w = WINDOW
    r = np.arange(3 * w)[:, None] - w
    c = np.arange(w)[None, :]
    band = np.abs(r - c) <= w
    first = band & (r >= 0)
    last = band & (r < w)
    table = np.stack([first, band, last]).astype(np.float32)
    return (1.0 - table) * NEG_INF


def _attn_b_call(sink, qt, k, vt):
    b, _, _, s = qt.shape
    w = WINDOW
    nb = s // w
    bias = jnp.asarray(_band_bias(), F32)

    def bias_map(bi, hk, n):
        return (jnp.where(n == 0, 0, jnp.where(n == nb - 1, 2, 1)), 0, 0)

    def k_spec(off):
        return pl.BlockSpec((1, 1, w, HEAD_DIM),
                            lambda bi, hk, n: (bi, hk, jnp.clip(n + off, 0, nb - 1), 0))

    def v_spec(off):
        return pl.BlockSpec((1, 1, V_ROWS, w),
                            lambda bi, hk, n: (bi, hk, 0, jnp.clip(n + off, 0, nb - 1)))

    return pl.pallas_call(
        _attn_b_kernel,
        out_shape=jax.ShapeDtypeStruct((b, s, Q_HEADS * HEAD_DIM), BF16),
        grid=(b, KV_HEADS, nb),
        in_specs=[
            pl.BlockSpec(memory_space=pltpu.SMEM),
            pl.BlockSpec((1, 3 * w, w), bias_map),
            pl.BlockSpec((1, GROUP, HEAD_DIM, w), lambda bi, hk, n: (bi, hk, 0, n)),
            k_spec(-1), k_spec(0), k_spec(1),
            v_spec(-1), v_spec(0), v_spec(1),
        ],
        out_specs=pl.BlockSpec((1, w, GROUP * HEAD_DIM), lambda bi, hk, n: (bi, n, hk)),
        compiler_params=pltpu.CompilerParams(
            dimension_semantics=("arbitrary", "arbitrary", "arbitrary"),
            vmem_limit_bytes=V7X_VMEM_LIMIT),
        name="attn_window",
    )(sink, bias, qt, k, k, k, vt, vt, vt)


def _merge_kernel(x_ref, mod_ref, ya_ref, yb_ref, sg_ref, wb_ref, wo_ref, o_ref):
    d = x_ref.shape[2]
    dot = functools.partial(jnp.dot, preferred_element_type=F32)
    ua = dot(ya_ref[0], wb_ref[0])
    ub = dot(yb_ref[0], wb_ref[1])
    sg = sg_ref[0]
    merged = sg[:, :d].astype(F32) * ua + sg[:, d:].astype(F32) * ub
    gate1 = mod_ref[0, 2:3, :]
    o_ref[0] = x_ref[0] + gate1 * dot(merged.astype(BF16), wo_ref[...])


def _merge_call(x, mod3, ya, yb, sg, wb, wo, *, tm=512):
    b, s, d = x.shape
    bw = ya.shape[2]
    tok = lambda bi, i: (bi, i, 0)
    return pl.pallas_call(
        _merge_kernel,
        out_shape=jax.ShapeDtypeStruct((b, s, d), F32),
        grid=(b, s // tm),
        in_specs=[
            pl.BlockSpec((1, tm, d), tok),
            pl.BlockSpec((1, 6, d), lambda bi, i: (bi, 0, 0)),
            pl.BlockSpec((1, tm, bw), tok),
            pl.BlockSpec((1, tm, bw), tok),
            pl.BlockSpec((1, tm, 2 * d), tok),
            pl.BlockSpec((2, bw, d), lambda bi, i: (0, 0, 0)),
            pl.BlockSpec((d, d), lambda bi, i: (0, 0)),
        ],
        out_specs=pl.BlockSpec((1, tm, d), tok),
        compiler_params=pltpu.CompilerParams(
            dimension_semantics=("arbitrary", "arbitrary"),
            vmem_limit_bytes=V7X_VMEM_LIMIT),
        name="merge_out_proj",
    )(x, mod3, ya, yb, sg, wb, wo)


def _mlp_kernel(x_ref, mod_ref, g2_ref, gf_ref, w1_ref, w2_ref, o_ref, *, n_chunks):
    x = x_ref[0]
    shift2 = mod_ref[0, 3:4, :]
    scale2 = mod_ref[0, 4:5, :]
    gate2 = mod_ref[0, 5:6, :]
    h2 = (_rms(x) * g2_ref[...] * (1.0 + scale2) + shift2).astype(BF16)
    ff = w1_ref.shape[1]
    cw = ff // n_chunks
    acc = jnp.zeros(x.shape, F32)
    for c in range(n_chunks):
        hid = jnp.dot(h2, w1_ref[:, c * cw:(c + 1) * cw], preferred_element_type=F32)
        hid = jnp.square(jnp.maximum(hid, 0.0)).astype(BF16)
        acc = acc + jnp.dot(hid, w2_ref[c * cw:(c + 1) * cw, :], preferred_element_type=F32)
    x2 = x + gate2 * acc
    o_ref[0] = _rms(x2) * gf_ref[...]


def _mlp_call(x, mod3, g2, gf, w1, w2, *, tm=512, n_chunks=4):
    b, s, d = x.shape
    ff = w1.shape[1]
    tok = lambda bi, i: (bi, i, 0)
    const2 = lambda bi, i: (0, 0)
    return pl.pallas_call(
        functools.partial(_mlp_kernel, n_chunks=n_chunks),
        out_shape=jax.ShapeDtypeStruct((b, s, d), F32),
        grid=(b, s // tm),
        in_specs=[
            pl.BlockSpec((1, tm, d), tok),
            pl.BlockSpec((1, 6, d), lambda bi, i: (bi, 0, 0)),
            pl.BlockSpec((1, d), const2),
            pl.BlockSpec((1, d), const2),
            pl.BlockSpec((d, ff), const2),
            pl.BlockSpec((ff, d), const2),
        ],
        out_specs=pl.BlockSpec((1, tm, d), tok),
        compiler_params=pltpu.CompilerParams(
            dimension_semantics=("arbitrary", "arbitrary"),
            vmem_limit_bytes=V7X_VMEM_LIMIT),
        name="mlp_final_norm",
    )(x, mod3, g2, gf, w1, w2)


def _rope_tables(s):
    def cos_sin(pos, dim, theta):
        inv = theta ** (-jnp.arange(0, dim, 2, dtype=F32) / dim)
        ang = pos.astype(F32)[:, None] * inv[None, :]
        return jnp.cos(ang).T, jnp.sin(ang).T

    t = jnp.arange(s, dtype=jnp.int32)
    cr, sr = cos_sin(t // GRID_W, HEAD_DIM // 2, AXIAL_THETA)
    cc, sc = cos_sin(t % GRID_W, HEAD_DIM // 2, AXIAL_THETA)
    c1, s1 = cos_sin(t, HEAD_DIM, ROPE_THETA)
    cos_a = jnp.concatenate([cr, cr, cc, cc], axis=0)
    sin_a = jnp.concatenate([-sr, sr, -sc, sc], axis=0)
    cos_b = jnp.concatenate([c1, c1], axis=0)
    sin_b = jnp.concatenate([-s1, s1], axis=0)
    return cos_a, sin_a, cos_b, sin_b


def kernel(x, c, w_ada, b_ada, norm1_g, w_in, q_norm_a, k_norm_a, sink_b, w_branch, w_out,
           norm2_g, w_mlp_in, w_mlp_out, final_g):
    b, s, d = x.shape
    assert w_ada.shape[0] == 1, "kernel supports DEPTH == 1"
    l = 0
    n_qkv = 2 * (Q_HEADS + 2 * KV_HEADS) * HEAD_DIM
    cos_a, sin_a, cos_b, sin_b = _rope_tables(s)
    c_pad = jnp.pad(c, ((0, 8 - b), (0, 0)))

    mod = _mod_call(c_pad, w_ada[l], b_ada[l][None, :])
    mod3 = mod[:b].reshape(b, 6, d)
    wt = w_in[l][:, :n_qkv].T.astype(BF16)
    wg = w_in[l][:, n_qkv:].astype(BF16)
    qa, ka, va, qb, kb, vb, sg = _proj_call(
        x, mod3, norm1_g[l][None, :], wt, wg,
        q_norm_a[l][:, None], k_norm_a[l][:, None], cos_a, sin_a, cos_b, sin_b)
    ya = _attn_a_call(qa, ka, va)
    yb = _attn_b_call(sink_b[l], qb, kb, vb)
    x1 = _merge_call(x, mod3, ya, yb, sg, w_branch[l].astype(BF16), w_out[l].astype(BF16))
    return _mlp_call(x1, mod3, norm2_g[l][None, :], final_g[None, :],
                     w_mlp_in[l].astype(BF16), w_mlp_out[l].astype(BF16))
```

```python
import functools
import math

import jax
import jax.numpy as jnp
import numpy as np
from jax import lax
from jax.experimental import pallas as pl
from jax.experimental.pallas import tpu as pltpu

F32 = jnp.float32
BF16 = jnp.bfloat16

HEAD_DIM = 64
Q_HEADS = 8
KV_HEADS = 2
GROUP = Q_HEADS // KV_HEADS
GRID_W = 64
WINDOW = 128
ROPE_THETA = 10000.0
AXIAL_THETA = 10000.0
NORM_EPS = 1e-6
NEG_INF = -1e30
LOG2E = math.log2(math.e)

V7X_BF16_SUBLANES = 16
V7X_VMEM_LIMIT = 56 * 1024 * 1024

V_ROWS = HEAD_DIM + V7X_BF16_SUBLANES


def _rms(x, eps=NORM_EPS):
    return x * lax.rsqrt(jnp.mean(x * x, axis=-1, keepdims=True) + eps)


def _mod_kernel(c_ref, w_ref, b_ref, o_ref):
    c = c_ref[...]
    a = c * (1.0 / (1.0 + jnp.exp(-c)))
    w = w_ref[...]
    a_hi = a.astype(BF16)
    a_lo = (a - a_hi.astype(F32)).astype(BF16)
    w_hi = w.astype(BF16)
    w_lo = (w - w_hi.astype(F32)).astype(BF16)
    dot = functools.partial(jnp.dot, preferred_element_type=F32)
    o_ref[...] = dot(a_hi, w_hi) + (dot(a_hi, w_lo) + dot(a_lo, w_hi)) + b_ref[...]


def _mod_call(c_pad, w_ada, b_ada, *, tn=1024):
    rows, d = c_pad.shape
    n = w_ada.shape[1]
    return pl.pallas_call(
        _mod_kernel,
        out_shape=jax.ShapeDtypeStruct((rows, n), F32),
        grid=(n // tn,),
        in_specs=[
            pl.BlockSpec((rows, d), lambda j: (0, 0)),
            pl.BlockSpec((d, tn), lambda j: (0, j)),
            pl.BlockSpec((1, tn), lambda j: (0, j)),
        ],
        out_specs=pl.BlockSpec((rows, tn), lambda j: (0, j)),
        compiler_params=pltpu.CompilerParams(
            dimension_semantics=("arbitrary",), vmem_limit_bytes=V7X_VMEM_LIMIT),
        name="adaln_mod",
    )(c_pad, w_ada, b_ada)


def _rope_t(x, cos, sin, half):
    parts = []
    for base in range(0, HEAD_DIM, 2 * half):
        parts.append(x[base + half:base + 2 * half])
        parts.append(x[base:base + half])
    x_rot = jnp.concatenate(parts, axis=0)
    return x * cos + x_rot * sin


def _proj_kernel(x_ref, mod_ref, g1_ref, wt_ref, wg_ref, gq_ref, gk_ref,
                 cos_a_ref, sin_a_ref, cos_b_ref, sin_b_ref,
                 qa_ref, ka_ref, va_ref, kna_ref, qb_ref, kb_ref, vb_ref, sg_ref):
    x = x_ref[0]
    shift1 = mod_ref[0, 0:1, :]
    scale1 = mod_ref[0, 1:2, :]
    h = (_rms(x) * g1_ref[...] * (1.0 + scale1) + shift1).astype(BF16)

    pt = lax.dot_general(wt_ref[...], h, (((1,), (1,)), ((), ())),
                         preferred_element_type=F32)
    tm = pt.shape[1]
    q_scale = HEAD_DIM ** -0.5 * LOG2E
    ones = jnp.ones((V_ROWS - HEAD_DIM, tm), BF16)

    cos_a, sin_a = cos_a_ref[...], sin_a_ref[...]
    cos_b, sin_b = cos_b_ref[...], sin_b_ref[...]
    gq, gk = gq_ref[...], gk_ref[...]

    def head(row):
        return pt[row:row + HEAD_DIM]

    def normed(t, g):
        ms = jnp.sum(t * t, axis=0, keepdims=True) * (1.0 / HEAD_DIM)
        return t * lax.rsqrt(ms + NORM_EPS) * g

    row = 0
    for hq in range(Q_HEADS):
        t = _rope_t(normed(head(row), gq), cos_a, sin_a, HEAD_DIM // 4)
        qa_ref[0, hq] = (t * q_scale).astype(BF16)
        row += HEAD_DIM
    for hk in range(KV_HEADS):
        t = _rope_t(normed(head(row), gk), cos_a, sin_a, HEAD_DIM // 4)
        ka_ref[0, hk] = t.T.astype(BF16)
        tr = t.astype(BF16).astype(F32)
        kna_ref[0, hk] = jnp.sum(tr * tr, axis=0, keepdims=True)
        row += HEAD_DIM
    for hk in range(KV_HEADS):
        va_ref[0, hk] = jnp.concatenate([head(row).astype(BF16), ones], axis=0)
        row += HEAD_DIM
    for hq in range(Q_HEADS):
        t = _rope_t(head(row), cos_b, sin_b, HEAD_DIM // 2)
        qb_ref[0, hq] = (t * q_scale).astype(BF16)
        row += HEAD_DIM
    for hk in range(KV_HEADS):
        t = _rope_t(head(row), cos_b, sin_b, HEAD_DIM // 2)
        kb_ref[0, hk] = t.T.astype(BF16)
        row += HEAD_DIM
    for hk in range(KV_HEADS):
        vb_ref[0, hk] = jnp.concatenate([head(row).astype(BF16), ones], axis=0)
        row += HEAD_DIM

    g = jnp.dot(h, wg_ref[...], preferred_element_type=F32)
    sg_ref[0] = (1.0 / (1.0 + jnp.exp(-g))).astype(BF16)


def _proj_call(x, mod3, g1, wt, wg, gq, gk, cos_a, sin_a, cos_b, sin_b, *, tm=512):
    b, s, d = x.shape
    n_qkv = wt.shape[0]
    n_g = wg.shape[1]
    const2 = lambda bi, i: (0, 0)
    tab = pl.BlockSpec((HEAD_DIM, tm), lambda bi, i: (0, i))
    q_shape = jax.ShapeDtypeStruct((b, Q_HEADS, HEAD_DIM, s), BF16)
    k_shape = jax.ShapeDtypeStruct((b, KV_HEADS, s, HEAD_DIM), BF16)
    v_shape = jax.ShapeDtypeStruct((b, KV_HEADS, V_ROWS, s), BF16)
    q_spec = pl.BlockSpec((1, Q_HEADS, HEAD_DIM, tm), lambda bi, i: (bi, 0, 0, i))
    k_spec = pl.BlockSpec((1, KV_HEADS, tm, HEAD_DIM), lambda bi, i: (bi, 0, i, 0))
    v_spec = pl.BlockSpec((1, KV_HEADS, V_ROWS, tm), lambda bi, i: (bi, 0, 0, i))
    kn_shape = jax.ShapeDtypeStruct((b, KV_HEADS, 1, s), F32)
    kn_spec = pl.BlockSpec((1, KV_HEADS, 1, tm), lambda bi, i: (bi, 0, 0, i))
    return pl.pallas_call(
        _proj_kernel,
        out_shape=(q_shape, k_shape, v_shape, kn_shape, q_shape, k_shape, v_shape,
                   jax.ShapeDtypeStruct((b, s, n_g), BF16)),
        grid=(b, s // tm),
        in_specs=[
            pl.BlockSpec((1, tm, d), lambda bi, i: (bi, i, 0)),
            pl.BlockSpec((1, 6, d), lambda bi, i: (bi, 0, 0)),
            pl.BlockSpec((1, d), const2),
            pl.BlockSpec((n_qkv, d), const2),
            pl.BlockSpec((d, n_g), const2),
            pl.BlockSpec((HEAD_DIM, 1), const2),
            pl.BlockSpec((HEAD_DIM, 1), const2),
            tab, tab, tab, tab,
        ],
        out_specs=(q_spec, k_spec, v_spec, kn_spec, q_spec, k_spec, v_spec,
                   pl.BlockSpec((1, tm, n_g), lambda bi, i: (bi, i, 0))),
        compiler_params=pltpu.CompilerParams(
            dimension_semantics=("arbitrary", "arbitrary"),
            vmem_limit_bytes=V7X_VMEM_LIMIT),
        name="in_proj",
    )(x, mod3, g1, wt, wg, gq, gk, cos_a, sin_a, cos_b, sin_b)


_MAX_FIXED_STABILISER = 50.0


def _attn_a_finish(acc, o_ref, tq):
    o = acc[:HEAD_DIM] * (1.0 / acc[HEAD_DIM:HEAD_DIM + 1])
    o_ref[0] = jnp.concatenate(
        [o[:, hq * tq:(hq + 1) * tq].T for hq in range(GROUP)], axis=1).astype(BF16)


def _attn_a_kernel(q_ref, k_ref, v_ref, kn_ref, o_ref, *, tk, unroll):
    tq = q_ref.shape[3]
    n = GROUP * tq
    n_chunks = k_ref.shape[2] // tk
    q = jnp.concatenate([q_ref[0, hq] for hq in range(GROUP)], axis=1)
    qf = q.astype(F32)
    qn2 = jnp.sum(qf * qf, axis=0, keepdims=True)
    kmax2 = jnp.max(kn_ref[0, 0], axis=1, keepdims=True)
    bound = jnp.sqrt(qn2 * kmax2)
    fixed_ok = jnp.max(bound) <= _MAX_FIXED_STABILISER

    def chunk(c):
        start = pl.multiple_of(c * tk, tk)
        return k_ref[0, 0, pl.ds(start, tk), :], v_ref[0, 0, :, pl.ds(start, tk)]

    @pl.when(fixed_ok)
    def _():
        def scores(c):
            return jnp.dot(chunk(c)[0], q, preferred_element_type=F32)

        def consume(st, c, acc):
            p = jnp.exp2(st - bound).astype(BF16)
            return acc + jnp.dot(chunk(c)[1], p, preferred_element_type=F32)

        def body(c, carry):
            st, acc = carry
            st_next = scores(c + 1)
            return st_next, consume(st, c, acc)

        init = (scores(0), jnp.zeros((V_ROWS, n), F32))
        st, acc = lax.fori_loop(0, n_chunks - 1, body, init, unroll=unroll)
        _attn_a_finish(consume(st, n_chunks - 1, acc), o_ref, tq)

    @pl.when(jnp.logical_not(fixed_ok))
    def _():
        def body(c, carry):
            m, acc = carry
            kc, vc = chunk(c)
            st = jnp.dot(kc, q, preferred_element_type=F32)
            m_new = jnp.maximum(m, jnp.max(st, axis=0, keepdims=True))
            p = jnp.exp2(st - m_new).astype(BF16)
            acc = acc * jnp.exp2(m - m_new) + jnp.dot(vc, p, preferred_element_type=F32)
            return m_new, acc

        init = (jnp.full((1, n), NEG_INF, F32), jnp.zeros((V_ROWS, n), F32))
        _, acc = lax.fori_loop(0, n_chunks, body, init)
        _attn_a_finish(acc, o_ref, tq)


def _attn_a_call(qt, k, vt, kn, *, tq=256, tk=256, unroll=True):
    b, _, _, s = qt.shape
    return pl.pallas_call(
        functools.partial(_attn_a_kernel, tk=tk, unroll=unroll),
        out_shape=jax.ShapeDtypeStruct((b, s, Q_HEADS * HEAD_DIM), BF16),
        grid=(b, KV_HEADS, s // tq),
        in_specs=[
            pl.BlockSpec((1, GROUP, HEAD_DIM, tq), lambda bi, hk, i: (bi, hk, 0, i)),
            pl.BlockSpec((1, 1, s, HEAD_DIM), lambda bi, hk, i: (bi, hk, 0, 0)),
            pl.BlockSpec((1, 1, V_ROWS, s), lambda bi, hk, i: (bi, hk, 0, 0)),
            pl.BlockSpec((1, 1, 1, s), lambda bi, hk, i: (bi, hk, 0, 0)),
        ],
        out_specs=pl.BlockSpec((1, tq, GROUP * HEAD_DIM), lambda bi, hk, i: (bi, i, hk)),
        compiler_params=pltpu.CompilerParams(
            dimension_semantics=("arbitrary", "arbitrary", "arbitrary"),
            vmem_limit_bytes=V7X_VMEM_LIMIT),
        name="attn_global",
    )(qt, k, vt, kn)


def _attn_b_kernel(sink_ref, bias_ref, q_ref, kl_ref, kc_ref, kr_ref,
                   vl_ref, vc_ref, vr_ref, o_ref):
    hk = pl.program_id(1)
    w = q_ref.shape[3]
    q = jnp.concatenate([q_ref[0, hq] for hq in range(GROUP)], axis=1)
    k = jnp.concatenate([kl_ref[0, 0], kc_ref[0, 0], kr_ref[0, 0]], axis=0)
    st = jnp.dot(k, q, preferred_element_type=F32)
    bias = bias_ref[0]
    st = st + jnp.concatenate([bias] * GROUP, axis=1)
    sink = jnp.concatenate(
        [jnp.full((1, w), sink_ref[hk * GROUP + hq] * LOG2E, F32) for hq in range(GROUP)],
        axis=1)
    m = jnp.maximum(jnp.max(st, axis=0, keepdims=True), sink)
    p = jnp.exp2(st - m).astype(BF16)
    v = jnp.concatenate([vl_ref[0, 0], vc_ref[0, 0], vr_ref[0, 0]], axis=1)
    acc = jnp.dot(v, p, preferred_element_type=F32)
    denom = acc[HEAD_DIM:HEAD_DIM + 1] + jnp.exp2(sink - m)
    o = acc[:HEAD_DIM] * (1.0 / denom)
    o_ref[0] = jnp.concatenate(
        [o[:, hq * w:(hq + 1) * w].T for hq in range(GROUP)], axis=1).astype(BF16)


def _band_bias():
    w = WINDOW
    r = np.arange(3 * w)[:, None] - w
    c = np.arange(w)[None, :]
    band = np.abs(r - c) <= w
    first = band & (r >= 0)
    last = band & (r < w)
    table = np.stack([first, band, last]).astype(np.float32)
    return (1.0 - table) * NEG_INF


def _attn_b_call(sink, qt, k, vt):
    b, _, _, s = qt.shape
    w = WINDOW
    nb = s // w
    bias = jnp.asarray(_band_bias(), F32)

    def bias_map(bi, hk, n):
        return (jnp.where(n == 0, 0, jnp.where(n == nb - 1, 2, 1)), 0, 0)

    def k_spec(off):
        return pl.BlockSpec((1, 1, w, HEAD_DIM),
                            lambda bi, hk, n: (bi, hk, jnp.clip(n + off, 0, nb - 1), 0))

    def v_spec(off):
        return pl.BlockSpec((1, 1, V_ROWS, w),
                            lambda bi, hk, n: (bi, hk, 0, jnp.clip(n + off, 0, nb - 1)))

    return pl.pallas_call(
        _attn_b_kernel,
        out_shape=jax.ShapeDtypeStruct((b, s, Q_HEADS * HEAD_DIM), BF16),
        grid=(b, KV_HEADS, nb),
        in_specs=[
            pl.BlockSpec(memory_space=pltpu.SMEM),
            pl.BlockSpec((1, 3 * w, w), bias_map),
            pl.BlockSpec((1, GROUP, HEAD_DIM, w), lambda bi, hk, n: (bi, hk, 0, n)),
            k_spec(-1), k_spec(0), k_spec(1),
            v_spec(-1), v_spec(0), v_spec(1),
        ],
        out_specs=pl.BlockSpec((1, w, GROUP * HEAD_DIM), lambda bi, hk, n: (bi, n, hk)),
        compiler_params=pltpu.CompilerParams(
            dimension_semantics=("arbitrary", "arbitrary", "arbitrary"),
            vmem_limit_bytes=V7X_VMEM_LIMIT),
        name="attn_window",
    )(sink, bias, qt, k, k, k, vt, vt, vt)


def _merge_kernel(x_ref, mod_ref, ya_ref, yb_ref, sg_ref, wb_ref, wo_ref, o_ref):
    d = x_ref.shape[2]
    dot = functools.partial(jnp.dot, preferred_element_type=F32)
    ua = dot(ya_ref[0], wb_ref[0])
    ub = dot(yb_ref[0], wb_ref[1])
    sg = sg_ref[0]
    merged = sg[:, :d].astype(F32) * ua + sg[:, d:].astype(F32) * ub
    gate1 = mod_ref[0, 2:3, :]
    o_ref[0] = x_ref[0] + gate1 * dot(merged.astype(BF16), wo_ref[...])


def _merge_call(x, mod3, ya, yb, sg, wb, wo, *, tm=512):
    b, s, d = x.shape
    bw = ya.shape[2]
    tok = lambda bi, i: (bi, i, 0)
    return pl.pallas_call(
        _merge_kernel,
        out_shape=jax.ShapeDtypeStruct((b, s, d), F32),
        grid=(b, s // tm),
        in_specs=[
            pl.BlockSpec((1, tm, d), tok),
            pl.BlockSpec((1, 6, d), lambda bi, i: (bi, 0, 0)),
            pl.BlockSpec((1, tm, bw), tok),
            pl.BlockSpec((1, tm, bw), tok),
            pl.BlockSpec((1, tm, 2 * d), tok),
            pl.BlockSpec((2, bw, d), lambda bi, i: (0, 0, 0)),
            pl.BlockSpec((d, d), lambda bi, i: (0, 0)),
        ],
        out_specs=pl.BlockSpec((1, tm, d), tok),
        compiler_params=pltpu.CompilerParams(
            dimension_semantics=("arbitrary", "arbitrary"),
            vmem_limit_bytes=V7X_VMEM_LIMIT),
        name="merge_out_proj",
    )(x, mod3, ya, yb, sg, wb, wo)


def _mlp_kernel(x_ref, mod_ref, g2_ref, gf_ref, w1_ref, w2_ref, o_ref, *, n_chunks):
    x = x_ref[0]
    shift2 = mod_ref[0, 3:4, :]
    scale2 = mod_ref[0, 4:5, :]
    gate2 = mod_ref[0, 5:6, :]
    h2 = (_rms(x) * g2_ref[...] * (1.0 + scale2) + shift2).astype(BF16)
    ff = w1_ref.shape[1]
    cw = ff // n_chunks
    acc = jnp.zeros(x.shape, F32)
    for c in range(n_chunks):
        hid = jnp.dot(h2, w1_ref[:, c * cw:(c + 1) * cw], preferred_element_type=F32)
        hid = jnp.square(jnp.maximum(hid, 0.0)).astype(BF16)
        acc = acc + jnp.dot(hid, w2_ref[c * cw:(c + 1) * cw, :], preferred_element_type=F32)
    x2 = x + gate2 * acc
    o_ref[0] = _rms(x2) * gf_ref[...]


def _mlp_call(x, mod3, g2, gf, w1, w2, *, tm=512, n_chunks=4):
    b, s, d = x.shape
    ff = w1.shape[1]
    tok = lambda bi, i: (bi, i, 0)
    const2 = lambda bi, i: (0, 0)
    return pl.pallas_call(
        functools.partial(_mlp_kernel, n_chunks=n_chunks),
        out_shape=jax.ShapeDtypeStruct((b, s, d), F32),
        grid=(b, s // tm),
        in_specs=[
            pl.BlockSpec((1, tm, d), tok),
            pl.BlockSpec((1, 6, d), lambda bi, i: (bi, 0, 0)),
            pl.BlockSpec((1, d), const2),
            pl.BlockSpec((1, d), const2),
            pl.BlockSpec((d, ff), const2),
            pl.BlockSpec((ff, d), const2),
        ],
        out_specs=pl.BlockSpec((1, tm, d), tok),
        compiler_params=pltpu.CompilerParams(
            dimension_semantics=("arbitrary", "arbitrary"),
            vmem_limit_bytes=V7X_VMEM_LIMIT),
        name="mlp_final_norm",
    )(x, mod3, g2, gf, w1, w2)


def _rope_tables(s):
    def cos_sin(pos, dim, theta):
        inv = theta ** (-jnp.arange(0, dim, 2, dtype=F32) / dim)
        ang = pos.astype(F32)[:, None] * inv[None, :]
        return jnp.cos(ang).T, jnp.sin(ang).T

    t = jnp.arange(s, dtype=jnp.int32)
    cr, sr = cos_sin(t // GRID_W, HEAD_DIM // 2, AXIAL_THETA)
    cc, sc = cos_sin(t % GRID_W, HEAD_DIM // 2, AXIAL_THETA)
    c1, s1 = cos_sin(t, HEAD_DIM, ROPE_THETA)
    cos_a = jnp.concatenate([cr, cr, cc, cc], axis=0)
    sin_a = jnp.concatenate([-sr, sr, -sc, sc], axis=0)
    cos_b = jnp.concatenate([c1, c1], axis=0)
    sin_b = jnp.concatenate([-s1, s1], axis=0)
    return cos_a, sin_a, cos_b, sin_b


def kernel(x, c, w_ada, b_ada, norm1_g, w_in, q_norm_a, k_norm_a, sink_b, w_branch, w_out,
           norm2_g, w_mlp_in, w_mlp_out, final_g):
    b, s, d = x.shape
    assert w_ada.shape[0] == 1, "kernel supports DEPTH == 1"
    l = 0
    n_qkv = 2 * (Q_HEADS + 2 * KV_HEADS) * HEAD_DIM
    cos_a, sin_a, cos_b, sin_b = _rope_tables(s)
    c_pad = jnp.pad(c, ((0, 8 - b), (0, 0)))

    mod = _mod_call(c_pad, w_ada[l], b_ada[l][None, :])
    mod3 = mod[:b].reshape(b, 6, d)
    wt = w_in[l][:, :n_qkv].T.astype(BF16)
    wg = w_in[l][:, n_qkv:].astype(BF16)
    qa, ka, va, kna, qb, kb, vb, sg = _proj_call(
        x, mod3, norm1_g[l][None, :], wt, wg,
        q_norm_a[l][:, None], k_norm_a[l][:, None], cos_a, sin_a, cos_b, sin_b)
    ya = _attn_a_call(qa, ka, va, kna)
    yb = _attn_b_call(sink_b[l], qb, kb, vb)
    x1 = _merge_call(x, mod3, ya, yb, sg, w_branch[l].astype(BF16), w_out[l].astype(BF16))
    return _mlp_call(x1, mod3, norm2_g[l][None, :], final_g[None, :],
                     w_mlp_in[l].astype(BF16), w_mlp_out[l].astype(BF16))
```

```python
import functools
import math

import jax
import jax.numpy as jnp
import numpy as np
from jax import lax
from jax.experimental import pallas as pl
from jax.experimental.pallas import tpu as pltpu

F32 = jnp.float32
BF16 = jnp.bfloat16

HEAD_DIM = 64
Q_HEADS = 8
KV_HEADS = 2
GROUP = Q_HEADS // KV_HEADS
GRID_W = 64
WINDOW = 128
ROPE_THETA = 10000.0
AXIAL_THETA = 10000.0
NORM_EPS = 1e-6
NEG_INF = -1e30
LOG2E = math.log2(math.e)

V7X_BF16_SUBLANES = 16
V7X_VMEM_LIMIT = 56 * 1024 * 1024

V_ROWS = HEAD_DIM + V7X_BF16_SUBLANES


def _rms(x, eps=NORM_EPS):
    return x * lax.rsqrt(jnp.mean(x * x, axis=-1, keepdims=True) + eps)


def _mod_kernel(c_ref, w_ref, b_ref, o_ref):
    c = c_ref[...]
    a = c * (1.0 / (1.0 + jnp.exp(-c)))
    w = w_ref[...]
    a_hi = a.astype(BF16)
    a_lo = (a - a_hi.astype(F32)).astype(BF16)
    w_hi = w.astype(BF16)
    w_lo = (w - w_hi.astype(F32)).astype(BF16)
    dot = functools.partial(jnp.dot, preferred_element_type=F32)
    o_ref[...] = dot(a_hi, w_hi) + (dot(a_hi, w_lo) + dot(a_lo, w_hi)) + b_ref[...]


def _mod_call(c_pad, w_ada, b_ada, *, tn=1024):
    rows, d = c_pad.shape
    n = w_ada.shape[1]
    return pl.pallas_call(
        _mod_kernel,
        out_shape=jax.ShapeDtypeStruct((rows, n), F32),
        grid=(n // tn,),
        in_specs=[
            pl.BlockSpec((rows, d), lambda j: (0, 0)),
            pl.BlockSpec((d, tn), lambda j: (0, j)),
            pl.BlockSpec((1, tn), lambda j: (0, j)),
        ],
        out_specs=pl.BlockSpec((rows, tn), lambda j: (0, j)),
        compiler_params=pltpu.CompilerParams(
            dimension_semantics=("arbitrary",), vmem_limit_bytes=V7X_VMEM_LIMIT),
        name="adaln_mod",
    )(c_pad, w_ada, b_ada)


def _rope_t(x, cos, sin, half):
    parts = []
    for base in range(0, HEAD_DIM, 2 * half):
        parts.append(x[base + half:base + 2 * half])
        parts.append(x[base:base + half])
    x_rot = jnp.concatenate(parts, axis=0)
    return x * cos + x_rot * sin


def _proj_kernel(x_ref, mod_ref, g1_ref, wt_ref, wg_ref, gq_ref, gk_ref,
                 cos_a_ref, sin_a_ref, cos_b_ref, sin_b_ref,
                 qa_ref, ka_ref, va_ref, kna_ref, qb_ref, kb_ref, vb_ref, sg_ref):
    x = x_ref[0]
    shift1 = mod_ref[0, 0:1, :]
    scale1 = mod_ref[0, 1:2, :]
    h = (_rms(x) * g1_ref[...] * (1.0 + scale1) + shift1).astype(BF16)

    pt = lax.dot_general(wt_ref[...], h, (((1,), (1,)), ((), ())),
                         preferred_element_type=F32)
    tm = pt.shape[1]
    q_scale = HEAD_DIM ** -0.5 * LOG2E
    ones = jnp.ones((V_ROWS - HEAD_DIM, tm), BF16)

    cos_a, sin_a = cos_a_ref[...], sin_a_ref[...]
    cos_b, sin_b = cos_b_ref[...], sin_b_ref[...]
    gq, gk = gq_ref[...], gk_ref[...]

    def head(row):
        return pt[row:row + HEAD_DIM]

    def normed(t, g):
        ms = jnp.sum(t * t, axis=0, keepdims=True) * (1.0 / HEAD_DIM)
        return t * lax.rsqrt(ms + NORM_EPS) * g

    row = 0
    for hq in range(Q_HEADS):
        t = _rope_t(normed(head(row), gq), cos_a, sin_a, HEAD_DIM // 4)
        qa_ref[0, hq] = (t * q_scale).astype(BF16)
        row += HEAD_DIM
    for hk in range(KV_HEADS):
        t = _rope_t(normed(head(row), gk), cos_a, sin_a, HEAD_DIM // 4)
        ka_ref[0, hk] = t.T.astype(BF16)
        tr = t.astype(BF16).astype(F32)
        kna_ref[0, hk] = jnp.sum(tr * tr, axis=0, keepdims=True)
        row += HEAD_DIM
    for hk in range(KV_HEADS):
        va_ref[0, hk] = jnp.concatenate([head(row).astype(BF16), ones], axis=0)
        row += HEAD_DIM
    for hq in range(Q_HEADS):
        t = _rope_t(head(row), cos_b, sin_b, HEAD_DIM // 2)
        qb_ref[0, hq] = (t * q_scale).astype(BF16)
        row += HEAD_DIM
    for hk in range(KV_HEADS):
        t = _rope_t(head(row), cos_b, sin_b, HEAD_DIM // 2)
        kb_ref[0, hk] = t.T.astype(BF16)
        row += HEAD_DIM
    for hk in range(KV_HEADS):
        vb_ref[0, hk] = jnp.concatenate([head(row).astype(BF16), ones], axis=0)
        row += HEAD_DIM

    g = jnp.dot(h, wg_ref[...], preferred_element_type=F32)
    sg_ref[0] = (1.0 / (1.0 + jnp.exp(-g))).astype(BF16)


def _proj_call(x, mod3, g1, wt, wg, gq, gk, cos_a, sin_a, cos_b, sin_b, *, tm=512):
    b, s, d = x.shape
    n_qkv = wt.shape[0]
    n_g = wg.shape[1]
    const2 = lambda bi, i: (0, 0)
    tab = pl.BlockSpec((HEAD_DIM, tm), lambda bi, i: (0, i))
    q_shape = jax.ShapeDtypeStruct((b, Q_HEADS, HEAD_DIM, s), BF16)
    k_shape = jax.ShapeDtypeStruct((b, KV_HEADS, s, HEAD_DIM), BF16)
    v_shape = jax.ShapeDtypeStruct((b, KV_HEADS, V_ROWS, s), BF16)
    q_spec = pl.BlockSpec((1, Q_HEADS, HEAD_DIM, tm), lambda bi, i: (bi, 0, 0, i))
    k_spec = pl.BlockSpec((1, KV_HEADS, tm, HEAD_DIM), lambda bi, i: (bi, 0, i, 0))
    v_spec = pl.BlockSpec((1, KV_HEADS, V_ROWS, tm), lambda bi, i: (bi, 0, 0, i))
    kn_shape = jax.ShapeDtypeStruct((b, KV_HEADS, 1, s), F32)
    kn_spec = pl.BlockSpec((1, KV_HEADS, 1, tm), lambda bi, i: (bi, 0, 0, i))
    return pl.pallas_call(
        _proj_kernel,
        out_shape=(q_shape, k_shape, v_shape, kn_shape, q_shape, k_shape, v_shape,
                   jax.ShapeDtypeStruct((b, s, n_g), BF16)),
        grid=(b, s // tm),
        in_specs=[
            pl.BlockSpec((1, tm, d), lambda bi, i: (bi, i, 0)),
            pl.BlockSpec((1, 6, d), lambda bi, i: (bi, 0, 0)),
            pl.BlockSpec((1, d), const2),
            pl.BlockSpec((n_qkv, d), const2),
            pl.BlockSpec((d, n_g), const2),
            pl.BlockSpec((HEAD_DIM, 1), const2),
            pl.BlockSpec((HEAD_DIM, 1), const2),
            tab, tab, tab, tab,
        ],
        out_specs=(q_spec, k_spec, v_spec, kn_spec, q_spec, k_spec, v_spec,
                   pl.BlockSpec((1, tm, n_g), lambda bi, i: (bi, i, 0))),
        compiler_params=pltpu.CompilerParams(
            dimension_semantics=("arbitrary", "arbitrary"),
            vmem_limit_bytes=V7X_VMEM_LIMIT),
        name="in_proj",
    )(x, mod3, g1, wt, wg, gq, gk, cos_a, sin_a, cos_b, sin_b)


_MAX_FIXED_STABILISER = 50.0


def _attn_a_finish(acc, o_ref, tq):
    o = acc[:HEAD_DIM] * (1.0 / acc[HEAD_DIM:HEAD_DIM + 1])
    o_ref[0] = jnp.concatenate(
        [o[:, hq * tq:(hq + 1) * tq].T for hq in range(GROUP)], axis=1).astype(BF16)


def _attn_a_kernel(q_ref, k_ref, v_ref, kn_ref, o_ref, *, tk, unroll):
    tq = q_ref.shape[3]
    n = GROUP * tq
    n_chunks = k_ref.shape[2] // tk
    q = jnp.concatenate([q_ref[0, hq] for hq in range(GROUP)], axis=1)
    qf = q.astype(F32)
    qn2 = jnp.sum(qf * qf, axis=0, keepdims=True)
    kmax2 = jnp.max(kn_ref[0, 0], axis=1, keepdims=True)
    bound = jnp.sqrt(qn2 * kmax2)
    fixed_ok = jnp.max(bound) <= _MAX_FIXED_STABILISER

    def chunk(c):
        start = pl.multiple_of(c * tk, tk)
        return k_ref[0, 0, pl.ds(start, tk), :], v_ref[0, 0, :, pl.ds(start, tk)]

    @pl.when(fixed_ok)
    def _():
        def scores(c):
            return jnp.dot(chunk(c)[0], q, preferred_element_type=F32)

        def consume(st, c, acc):
            p = jnp.exp2(st - bound).astype(BF16)
            return acc + jnp.dot(chunk(c)[1], p, preferred_element_type=F32)

        def body(c, carry):
            st, acc = carry
            st_next = scores(c + 1)
            return st_next, consume(st, c, acc)

        init = (scores(0), jnp.zeros((V_ROWS, n), F32))
        st, acc = lax.fori_loop(0, n_chunks - 1, body, init, unroll=unroll)
        _attn_a_finish(consume(st, n_chunks - 1, acc), o_ref, tq)

    @pl.when(jnp.logical_not(fixed_ok))
    def _():
        def body(c, carry):
            m, acc = carry
            kc, vc = chunk(c)
            st = jnp.dot(kc, q, preferred_element_type=F32)
            m_new = jnp.maximum(m, jnp.max(st, axis=0, keepdims=True))
            p = jnp.exp2(st - m_new).astype(BF16)
            acc = acc * jnp.exp2(m - m_new) + jnp.dot(vc, p, preferred_element_type=F32)
            return m_new, acc

        init = (jnp.full((1, n), NEG_INF, F32), jnp.zeros((V_ROWS, n), F32))
        _, acc = lax.fori_loop(0, n_chunks, body, init)
        _attn_a_finish(acc, o_ref, tq)


def _attn_a_call(qt, k, vt, kn, *, tq=256, tk=256, unroll=True):
    b, _, _, s = qt.shape
    return pl.pallas_call(
        functools.partial(_attn_a_kernel, tk=tk, unroll=unroll),
        out_shape=jax.ShapeDtypeStruct((b, s, Q_HEADS * HEAD_DIM), BF16),
        grid=(b, KV_HEADS, s // tq),
        in_specs=[
            pl.BlockSpec((1, GROUP, HEAD_DIM, tq), lambda bi, hk, i: (bi, hk, 0, i)),
            pl.BlockSpec((1, 1, s, HEAD_DIM), lambda bi, hk, i: (bi, hk, 0, 0)),
            pl.BlockSpec((1, 1, V_ROWS, s), lambda bi, hk, i: (bi, hk, 0, 0)),
            pl.BlockSpec((1, 1, 1, s), lambda bi, hk, i: (bi, hk, 0, 0)),
        ],
        out_specs=pl.BlockSpec((1, tq, GROUP * HEAD_DIM), lambda bi, hk, i: (bi, i, hk)),
        compiler_params=pltpu.CompilerParams(
            dimension_semantics=("arbitrary", "arbitrary", "arbitrary"),
            vmem_limit_bytes=V7X_VMEM_LIMIT),
        name="attn_global",
    )(qt, k, vt, kn)


def _attn_b_kernel(sink_ref, bias_ref, q_ref, kl_ref, kc_ref, kr_ref,
                   vl_ref, vc_ref, vr_ref, o_ref, *, nsub):
    hk = pl.program_id(1)
    i = pl.program_id(2)
    last_i = pl.num_programs(2) - 1
    w = kl_ref.shape[2]
    k = jnp.concatenate([kl_ref[0, 0], kc_ref[0, 0], kr_ref[0, 0]], axis=0)
    v = jnp.concatenate([vl_ref[0, 0], vc_ref[0, 0], vr_ref[0, 0]], axis=1)
    sink = jnp.concatenate(
        [jnp.full((1, w), sink_ref[hk * GROUP + hq] * LOG2E, F32) for hq in range(GROUP)],
        axis=1)
    band = bias_ref[1]

    def scores(j):
        q = jnp.concatenate([q_ref[0, hq, :, j * w:(j + 1) * w] for hq in range(GROUP)], axis=1)
        return jnp.dot(k[j * w:(j + 3) * w], q, preferred_element_type=F32)

    st_next = scores(0)
    for j in range(nsub):
        st, st_next = st_next, (scores(j + 1) if j + 1 < nsub else None)
        bias = band
        if j == 0:
            bias = jnp.where(i == 0, bias_ref[0], band)
        if j == nsub - 1:
            bias = jnp.where(i == last_i, bias_ref[2], bias)
        st = st + jnp.concatenate([bias] * GROUP, axis=1)
        m = jnp.maximum(jnp.max(st, axis=0, keepdims=True), sink)
        p = jnp.exp2(st - m).astype(BF16)
        acc = jnp.dot(v[:, j * w:(j + 3) * w], p, preferred_element_type=F32)
        denom = acc[HEAD_DIM:HEAD_DIM + 1] + jnp.exp2(sink - m)
        o = acc[:HEAD_DIM] * (1.0 / denom)
        o_ref[0, j * w:(j + 1) * w, :] = jnp.concatenate(
            [o[:, hq * w:(hq + 1) * w].T for hq in range(GROUP)], axis=1).astype(BF16)


def _band_bias():
    w = WINDOW
    r = np.arange(3 * w)[:, None] - w
    c = np.arange(w)[None, :]
    band = np.abs(r - c) <= w
    first = band & (r >= 0)
    last = band & (r < w)
    table = np.stack([first, band, last]).astype(np.float32)
    return (1.0 - table) * NEG_INF


def _attn_b_call(sink, qt, k, vt, *, nsub=8):
    b, _, _, s = qt.shape
    w = WINDOW
    nb = s // w
    span = nsub * w
    bias = jnp.asarray(_band_bias(), F32)
    left = lambda i: jnp.maximum(i * nsub - 1, 0)
    right = lambda i: jnp.minimum((i + 1) * nsub, nb - 1)

    return pl.pallas_call(
        functools.partial(_attn_b_kernel, nsub=nsub),
        out_shape=jax.ShapeDtypeStruct((b, s, Q_HEADS * HEAD_DIM), BF16),
        grid=(b, KV_HEADS, s // span),
        in_specs=[
            pl.BlockSpec(memory_space=pltpu.SMEM),
            pl.BlockSpec((3, 3 * w, w), lambda bi, hk, i: (0, 0, 0)),
            pl.BlockSpec((1, GROUP, HEAD_DIM, span), lambda bi, hk, i: (bi, hk, 0, i)),
            pl.BlockSpec((1, 1, w, HEAD_DIM), lambda bi, hk, i: (bi, hk, left(i), 0)),
            pl.BlockSpec((1, 1, span, HEAD_DIM), lambda bi, hk, i: (bi, hk, i, 0)),
            pl.BlockSpec((1, 1, w, HEAD_DIM), lambda bi, hk, i: (bi, hk, right(i), 0)),
            pl.BlockSpec((1, 1, V_ROWS, w), lambda bi, hk, i: (bi, hk, 0, left(i))),
            pl.BlockSpec((1, 1, V_ROWS, span), lambda bi, hk, i: (bi, hk, 0, i)),
            pl.BlockSpec((1, 1, V_ROWS, w), lambda bi, hk, i: (bi, hk, 0, right(i))),
        ],
        out_specs=pl.BlockSpec((1, span, GROUP * HEAD_DIM), lambda bi, hk, i: (bi, i, hk)),
        compiler_params=pltpu.CompilerParams(
            dimension_semantics=("arbitrary", "arbitrary", "arbitrary"),
            vmem_limit_bytes=V7X_VMEM_LIMIT),
        name="attn_window",
    )(sink, bias, qt, k, k, k, vt, vt, vt)


def _merge_kernel(x_ref, mod_ref, ya_ref, yb_ref, sg_ref, wb_ref, wo_ref, o_ref):
    d = x_ref.shape[2]
    dot = functools.partial(jnp.dot, preferred_element_type=F32)
    ua = dot(ya_ref[0], wb_ref[0])
    ub = dot(yb_ref[0], wb_ref[1])
    sg = sg_ref[0]
    merged = sg[:, :d].astype(F32) * ua + sg[:, d:].astype(F32) * ub
    gate1 = mod_ref[0, 2:3, :]
    o_ref[0] = x_ref[0] + gate1 * dot(merged.astype(BF16), wo_ref[...])


def _merge_call(x, mod3, ya, yb, sg, wb, wo, *, tm=512):
    b, s, d = x.shape
    bw = ya.shape[2]
    tok = lambda bi, i: (bi, i, 0)
    return pl.pallas_call(
        _merge_kernel,
        out_shape=jax.ShapeDtypeStruct((b, s, d), F32),
        grid=(b, s // tm),
        in_specs=[
            pl.BlockSpec((1, tm, d), tok),
            pl.BlockSpec((1, 6, d), lambda bi, i: (bi, 0, 0)),
            pl.BlockSpec((1, tm, bw), tok),
            pl.BlockSpec((1, tm, bw), tok),
            pl.BlockSpec((1, tm, 2 * d), tok),
            pl.BlockSpec((2, bw, d), lambda bi, i: (0, 0, 0)),
            pl.BlockSpec((d, d), lambda bi, i: (0, 0)),
        ],
        out_specs=pl.BlockSpec((1, tm, d), tok),
        compiler_params=pltpu.CompilerParams(
            dimension_semantics=("arbitrary", "arbitrary"),
            vmem_limit_bytes=V7X_VMEM_LIMIT),
        name="merge_out_proj",
    )(x, mod3, ya, yb, sg, wb, wo)


def _mlp_kernel(x_ref, mod_ref, g2_ref, gf_ref, w1_ref, w2_ref, o_ref, *, n_chunks):
    x = x_ref[0]
    shift2 = mod_ref[0, 3:4, :]
    scale2 = mod_ref[0, 4:5, :]
    gate2 = mod_ref[0, 5:6, :]
    h2 = (_rms(x) * g2_ref[...] * (1.0 + scale2) + shift2).astype(BF16)
    ff = w1_ref.shape[1]
    cw = ff // n_chunks
    acc = jnp.zeros(x.shape, F32)
    for c in range(n_chunks):
        hid = jnp.dot(h2, w1_ref[:, c * cw:(c + 1) * cw], preferred_element_type=F32)
        hid = jnp.square(jnp.maximum(hid, 0.0)).astype(BF16)
        acc = acc + jnp.dot(hid, w2_ref[c * cw:(c + 1) * cw, :], preferred_element_type=F32)
    x2 = x + gate2 * acc
    o_ref[0] = _rms(x2) * gf_ref[...]


def _mlp_call(x, mod3, g2, gf, w1, w2, *, tm=512, n_chunks=4):
    b, s, d = x.shape
    ff = w1.shape[1]
    tok = lambda bi, i: (bi, i, 0)
    const2 = lambda bi, i: (0, 0)
    return pl.pallas_call(
        functools.partial(_mlp_kernel, n_chunks=n_chunks),
        out_shape=jax.ShapeDtypeStruct((b, s, d), F32),
        grid=(b, s // tm),
        in_specs=[
            pl.BlockSpec((1, tm, d), tok),
            pl.BlockSpec((1, 6, d), lambda bi, i: (bi, 0, 0)),
            pl.BlockSpec((1, d), const2),
            pl.BlockSpec((1, d), const2),
            pl.BlockSpec((d, ff), const2),
            pl.BlockSpec((ff, d), const2),
        ],
        out_specs=pl.BlockSpec((1, tm, d), tok),
        compiler_params=pltpu.CompilerParams(
            dimension_semantics=("arbitrary", "arbitrary"),
            vmem_limit_bytes=V7X_VMEM_LIMIT),
        name="mlp_final_norm",
    )(x, mod3, g2, gf, w1, w2)


def _rope_tables(s):
    def cos_sin(pos, dim, theta):
        inv = theta ** (-jnp.arange(0, dim, 2, dtype=F32) / dim)
        ang = pos.astype(F32)[:, None] * inv[None, :]
        return jnp.cos(ang).T, jnp.sin(ang).T

    t = jnp.arange(s, dtype=jnp.int32)
    cr, sr = cos_sin(t // GRID_W, HEAD_DIM // 2, AXIAL_THETA)
    cc, sc = cos_sin(t % GRID_W, HEAD_DIM // 2, AXIAL_THETA)
    c1, s1 = cos_sin(t, HEAD_DIM, ROPE_THETA)
    cos_a = jnp.concatenate([cr, cr, cc, cc], axis=0)
    sin_a = jnp.concatenate([-sr, sr, -sc, sc], axis=0)
    cos_b = jnp.concatenate([c1, c1], axis=0)
    sin_b = jnp.concatenate([-s1, s1], axis=0)
    return cos_a, sin_a, cos_b, sin_b


def kernel(x, c, w_ada, b_ada, norm1_g, w_in, q_norm_a, k_norm_a, sink_b, w_branch, w_out,
           norm2_g, w_mlp_in, w_mlp_out, final_g):
    b, s, d = x.shape
    assert w_ada.shape[0] == 1, "kernel supports DEPTH == 1"
    l = 0
    n_qkv = 2 * (Q_HEADS + 2 * KV_HEADS) * HEAD_DIM
    cos_a, sin_a, cos_b, sin_b = _rope_tables(s)
    c_pad = jnp.pad(c, ((0, 8 - b), (0, 0)))

    mod = _mod_call(c_pad, w_ada[l], b_ada[l][None, :])
    mod3 = mod[:b].reshape(b, 6, d)
    wt = w_in[l][:, :n_qkv].T.astype(BF16)
    wg = w_in[l][:, n_qkv:].astype(BF16)
    qa, ka, va, kna, qb, kb, vb, sg = _proj_call(
        x, mod3, norm1_g[l][None, :], wt, wg,
        q_norm_a[l][:, None], k_norm_a[l][:, None], cos_a, sin_a, cos_b, sin_b)
    ya = _attn_a_call(qa, ka, va, kna)
    yb = _attn_b_call(sink_b[l], qb, kb, vb)
    x1 = _merge_call(x, mod3, ya, yb, sg, w_branch[l].astype(BF16), w_out[l].astype(BF16))
    return _mlp_call(x1, mod3, norm2_g[l][None, :], final_g[None, :],
                     w_mlp_in[l].astype(BF16), w_mlp_out[l].astype(BF16))
```

```python
import functools
import math

import jax
import jax.numpy as jnp
import numpy as np
from jax import lax
from jax.experimental import pallas as pl
from jax.experimental.pallas import tpu as pltpu

F32 = jnp.float32
BF16 = jnp.bfloat16

HEAD_DIM = 64
Q_HEADS = 8
KV_HEADS = 2
GROUP = Q_HEADS // KV_HEADS
GRID_W = 64
WINDOW = 128
ROPE_THETA = 10000.0
AXIAL_THETA = 10000.0
NORM_EPS = 1e-6
NEG_INF = -1e30
LOG2E = math.log2(math.e)

V7X_BF16_SUBLANES = 16
V7X_VMEM_LIMIT = 56 * 1024 * 1024

V_ROWS = HEAD_DIM + V7X_BF16_SUBLANES


def _rms(x, eps=NORM_EPS):
    return x * lax.rsqrt(jnp.mean(x * x, axis=-1, keepdims=True) + eps)


def _mod_kernel(c_ref, w_ref, b_ref, o_ref):
    c = c_ref[...]
    a = c * (1.0 / (1.0 + jnp.exp(-c)))
    w = w_ref[...]
    a_hi = a.astype(BF16)
    a_lo = (a - a_hi.astype(F32)).astype(BF16)
    w_hi = w.astype(BF16)
    w_lo = (w - w_hi.astype(F32)).astype(BF16)
    dot = functools.partial(jnp.dot, preferred_element_type=F32)
    o_ref[...] = dot(a_hi, w_hi) + (dot(a_hi, w_lo) + dot(a_lo, w_hi)) + b_ref[...]


def _mod_call(c_pad, w_ada, b_ada, *, tn=1024):
    rows, d = c_pad.shape
    n = w_ada.shape[1]
    return pl.pallas_call(
        _mod_kernel,
        out_shape=jax.ShapeDtypeStruct((rows, n), F32),
        grid=(n // tn,),
        in_specs=[
            pl.BlockSpec((rows, d), lambda j: (0, 0)),
            pl.BlockSpec((d, tn), lambda j: (0, j)),
            pl.BlockSpec((1, tn), lambda j: (0, j)),
        ],
        out_specs=pl.BlockSpec((rows, tn), lambda j: (0, j)),
        compiler_params=pltpu.CompilerParams(
            dimension_semantics=("arbitrary",), vmem_limit_bytes=V7X_VMEM_LIMIT),
        name="adaln_mod",
    )(c_pad, w_ada, b_ada)


def _rope_t(x, cos, sin, half):
    parts = []
    for base in range(0, HEAD_DIM, 2 * half):
        parts.append(x[base + half:base + 2 * half])
        parts.append(x[base:base + half])
    x_rot = jnp.concatenate(parts, axis=0)
    return x * cos + x_rot * sin


def _proj_kernel(x_ref, mod_ref, g1_ref, wt_ref, wg_ref, gq_ref, gk_ref,
                 cos_a_ref, sin_a_ref, cos_b_ref, sin_b_ref,
                 qa_ref, ka_ref, va_ref, qna_ref, kna_ref,
                 qb_ref, kb_ref, vb_ref, qnb_ref, knb_ref, sg_ref):
    x = x_ref[0]
    shift1 = mod_ref[0, 0:1, :]
    scale1 = mod_ref[0, 1:2, :]
    h = (_rms(x) * g1_ref[...] * (1.0 + scale1) + shift1).astype(BF16)

    pt = lax.dot_general(wt_ref[...], h, (((1,), (1,)), ((), ())),
                         preferred_element_type=F32)
    tm = pt.shape[1]
    q_scale = HEAD_DIM ** -0.5 * LOG2E
    ones = jnp.ones((V_ROWS - HEAD_DIM, tm), BF16)

    cos_a, sin_a = cos_a_ref[...], sin_a_ref[...]
    cos_b, sin_b = cos_b_ref[...], sin_b_ref[...]
    gq, gk = gq_ref[...], gk_ref[...]

    def head(row):
        return pt[row:row + HEAD_DIM]

    def normed(t, g):
        ms = jnp.sum(t * t, axis=0, keepdims=True) * (1.0 / HEAD_DIM)
        return t * lax.rsqrt(ms + NORM_EPS) * g

    def norm2(t_bf16):
        tr = t_bf16.astype(F32)
        return jnp.sum(tr * tr, axis=0, keepdims=True)

    row = 0
    for hq in range(Q_HEADS):
        t = _rope_t(normed(head(row), gq), cos_a, sin_a, HEAD_DIM // 4)
        tb = (t * q_scale).astype(BF16)
        qa_ref[0, hq] = tb
        qna_ref[0, hq] = norm2(tb)
        row += HEAD_DIM
    for hk in range(KV_HEADS):
        t = _rope_t(normed(head(row), gk), cos_a, sin_a, HEAD_DIM // 4)
        ka_ref[0, hk] = t.T.astype(BF16)
        kna_ref[0, hk] = norm2(t.astype(BF16))
        row += HEAD_DIM
    for hk in range(KV_HEADS):
        va_ref[0, hk] = jnp.concatenate([head(row).astype(BF16), ones], axis=0)
        row += HEAD_DIM
    for hq in range(Q_HEADS):
        t = _rope_t(head(row), cos_b, sin_b, HEAD_DIM // 2)
        tb = (t * q_scale).astype(BF16)
        qb_ref[0, hq] = tb
        qnb_ref[0, hq] = norm2(tb)
        row += HEAD_DIM
    for hk in range(KV_HEADS):
        t = _rope_t(head(row), cos_b, sin_b, HEAD_DIM // 2)
        kb_ref[0, hk] = t.T.astype(BF16)
        knb_ref[0, hk] = norm2(t.astype(BF16))
        row += HEAD_DIM
    for hk in range(KV_HEADS):
        vb_ref[0, hk] = jnp.concatenate([head(row).astype(BF16), ones], axis=0)
        row += HEAD_DIM

    g = jnp.dot(h, wg_ref[...], preferred_element_type=F32)
    sg_ref[0] = (1.0 / (1.0 + jnp.exp(-g))).astype(BF16)


def _proj_call(x, mod3, g1, wt, wg, gq, gk, cos_a, sin_a, cos_b, sin_b, *, tm=512):
    b, s, d = x.shape
    n_qkv = wt.shape[0]
    n_g = wg.shape[1]
    const2 = lambda bi, i: (0, 0)
    tab = pl.BlockSpec((HEAD_DIM, tm), lambda bi, i: (0, i))
    q_shape = jax.ShapeDtypeStruct((b, Q_HEADS, HEAD_DIM, s), BF16)
    k_shape = jax.ShapeDtypeStruct((b, KV_HEADS, s, HEAD_DIM), BF16)
    v_shape = jax.ShapeDtypeStruct((b, KV_HEADS, V_ROWS, s), BF16)
    q_spec = pl.BlockSpec((1, Q_HEADS, HEAD_DIM, tm), lambda bi, i: (bi, 0, 0, i))
    k_spec = pl.BlockSpec((1, KV_HEADS, tm, HEAD_DIM), lambda bi, i: (bi, 0, i, 0))
    v_spec = pl.BlockSpec((1, KV_HEADS, V_ROWS, tm), lambda bi, i: (bi, 0, 0, i))
    qn_shape = jax.ShapeDtypeStruct((b, Q_HEADS, 1, s), F32)
    qn_spec = pl.BlockSpec((1, Q_HEADS, 1, tm), lambda bi, i: (bi, 0, 0, i))
    kn_shape = jax.ShapeDtypeStruct((b, KV_HEADS, 1, s), F32)
    kn_spec = pl.BlockSpec((1, KV_HEADS, 1, tm), lambda bi, i: (bi, 0, 0, i))
    mixer_shapes = (q_shape, k_shape, v_shape, qn_shape, kn_shape)
    mixer_specs = (q_spec, k_spec, v_spec, qn_spec, kn_spec)
    return pl.pallas_call(
        _proj_kernel,
        out_shape=mixer_shapes + mixer_shapes + (jax.ShapeDtypeStruct((b, s, n_g), BF16),),
        grid=(b, s // tm),
        in_specs=[
            pl.BlockSpec((1, tm, d), lambda bi, i: (bi, i, 0)),
            pl.BlockSpec((1, 6, d), lambda bi, i: (bi, 0, 0)),
            pl.BlockSpec((1, d), const2),
            pl.BlockSpec((n_qkv, d), const2),
            pl.BlockSpec((d, n_g), const2),
            pl.BlockSpec((HEAD_DIM, 1), const2),
            pl.BlockSpec((HEAD_DIM, 1), const2),
            tab, tab, tab, tab,
        ],
        out_specs=mixer_specs + mixer_specs + (
            pl.BlockSpec((1, tm, n_g), lambda bi, i: (bi, i, 0)),),
        compiler_params=pltpu.CompilerParams(
            dimension_semantics=("arbitrary", "arbitrary"),
            vmem_limit_bytes=V7X_VMEM_LIMIT),
        name="in_proj",
    )(x, mod3, g1, wt, wg, gq, gk, cos_a, sin_a, cos_b, sin_b)


_MAX_FIXED_STABILISER = 50.0


def _attn_a_finish(acc, o_ref, tq):
    o = acc[:HEAD_DIM] * (1.0 / acc[HEAD_DIM:HEAD_DIM + 1])
    o_ref[0] = jnp.concatenate(
        [o[:, hq * tq:(hq + 1) * tq].T for hq in range(GROUP)], axis=1).astype(BF16)


def _bound_stats(qn, kn):
    b = qn.shape[0]
    kmax2 = jnp.max(kn, axis=(2, 3))
    qmax2 = jnp.max(qn.reshape(b, KV_HEADS, -1), axis=2)
    return jnp.stack([kmax2, qmax2 * kmax2], axis=-1).reshape(-1)


def _read_bound_stats(stats_ref):
    base = (pl.program_id(0) * KV_HEADS + pl.program_id(1)) * 2
    return stats_ref[base], stats_ref[base + 1] <= _MAX_FIXED_STABILISER ** 2


def _attn_a_kernel(stats_ref, q_ref, k_ref, v_ref, qn_ref, o_ref, *, tk, unroll):
    tq = q_ref.shape[3]
    n = GROUP * tq
    n_chunks = k_ref.shape[2] // tk
    q = jnp.concatenate([q_ref[0, hq] for hq in range(GROUP)], axis=1)
    kmax2, fixed_ok = _read_bound_stats(stats_ref)

    def chunk(c):
        start = pl.multiple_of(c * tk, tk)
        return k_ref[0, 0, pl.ds(start, tk), :], v_ref[0, 0, :, pl.ds(start, tk)]

    @pl.when(fixed_ok)
    def _():
        qn2 = jnp.concatenate([qn_ref[0, hq] for hq in range(GROUP)], axis=1)
        bound = jnp.sqrt(qn2 * kmax2)

        def scores(c):
            return jnp.dot(chunk(c)[0], q, preferred_element_type=F32)

        def consume(st, c, acc):
            p = jnp.exp2(st - bound).astype(BF16)
            return acc + jnp.dot(chunk(c)[1], p, preferred_element_type=F32)

        def body(c, carry):
            st, acc = carry
            st_next = scores(c + 1)
            return st_next, consume(st, c, acc)

        init = (scores(0), jnp.zeros((V_ROWS, n), F32))
        st, acc = lax.fori_loop(0, n_chunks - 1, body, init, unroll=unroll)
        _attn_a_finish(consume(st, n_chunks - 1, acc), o_ref, tq)

    @pl.when(jnp.logical_not(fixed_ok))
    def _():
        def body(c, carry):
            m, acc = carry
            kc, vc = chunk(c)
            st = jnp.dot(kc, q, preferred_element_type=F32)
            m_new = jnp.maximum(m, jnp.max(st, axis=0, keepdims=True))
            p = jnp.exp2(st - m_new).astype(BF16)
            acc = acc * jnp.exp2(m - m_new) + jnp.dot(vc, p, preferred_element_type=F32)
            return m_new, acc

        init = (jnp.full((1, n), NEG_INF, F32), jnp.zeros((V_ROWS, n), F32))
        _, acc = lax.fori_loop(0, n_chunks, body, init)
        _attn_a_finish(acc, o_ref, tq)


def _attn_a_call(qt, k, vt, qn, kn, *, tq=256, tk=256, unroll=True):
    b, _, _, s = qt.shape
    return pl.pallas_call(
        functools.partial(_attn_a_kernel, tk=tk, unroll=unroll),
        out_shape=jax.ShapeDtypeStruct((b, s, Q_HEADS * HEAD_DIM), BF16),
        grid=(b, KV_HEADS, s // tq),
        in_specs=[
            pl.BlockSpec(memory_space=pltpu.SMEM),
            pl.BlockSpec((1, GROUP, HEAD_DIM, tq), lambda bi, hk, i: (bi, hk, 0, i)),
            pl.BlockSpec((1, 1, s, HEAD_DIM), lambda bi, hk, i: (bi, hk, 0, 0)),
            pl.BlockSpec((1, 1, V_ROWS, s), lambda bi, hk, i: (bi, hk, 0, 0)),
            pl.BlockSpec((1, GROUP, 1, tq), lambda bi, hk, i: (bi, hk, 0, i)),
        ],
        out_specs=pl.BlockSpec((1, tq, GROUP * HEAD_DIM), lambda bi, hk, i: (bi, i, hk)),
        compiler_params=pltpu.CompilerParams(
            dimension_semantics=("arbitrary", "arbitrary", "arbitrary"),
            vmem_limit_bytes=V7X_VMEM_LIMIT),
        name="attn_global",
    )(_bound_stats(qn, kn), qt, k, vt, qn)


def _attn_b_kernel(sink_ref, stats_ref, bias_ref, q_ref, kl_ref, kc_ref, kr_ref,
                   vl_ref, vc_ref, vr_ref, qn_ref, o_ref, *, nsub):
    hk = pl.program_id(1)
    i = pl.program_id(2)
    last_i = pl.num_programs(2) - 1
    w = kl_ref.shape[2]
    k = jnp.concatenate([kl_ref[0, 0], kc_ref[0, 0], kr_ref[0, 0]], axis=0)
    v = jnp.concatenate([vl_ref[0, 0], vc_ref[0, 0], vr_ref[0, 0]], axis=1)
    sink = jnp.concatenate(
        [jnp.full((1, w), sink_ref[hk * GROUP + hq] * LOG2E, F32) for hq in range(GROUP)],
        axis=1)
    band = bias_ref[1]
    kmax2, fixed_ok = _read_bound_stats(stats_ref)

    def block_lanes(ref, j):
        return jnp.concatenate([ref[0, hq, :, j * w:(j + 1) * w] for hq in range(GROUP)], axis=1)

    def masked_scores(j):
        bias = band
        if j == 0:
            bias = jnp.where(i == 0, bias_ref[0], band)
        if j == nsub - 1:
            bias = jnp.where(i == last_i, bias_ref[2], bias)
        st = jnp.dot(k[j * w:(j + 3) * w], block_lanes(q_ref, j),
                     preferred_element_type=F32)
        return st + jnp.concatenate([bias] * GROUP, axis=1)

    def finish(j, st, m):
        p = jnp.exp2(st - m).astype(BF16)
        acc = jnp.dot(v[:, j * w:(j + 3) * w], p, preferred_element_type=F32)
        denom = acc[HEAD_DIM:HEAD_DIM + 1] + jnp.exp2(sink - m)
        o = acc[:HEAD_DIM] * (1.0 / denom)
        o_ref[0, j * w:(j + 1) * w, :] = jnp.concatenate(
            [o[:, hq * w:(hq + 1) * w].T for hq in range(GROUP)], axis=1).astype(BF16)

    @pl.when(fixed_ok)
    def _():
        ahead = 2
        sts = [masked_scores(j) for j in range(min(ahead, nsub))]
        for j in range(nsub):
            if j + ahead < nsub:
                sts.append(masked_scores(j + ahead))
            bound = jnp.sqrt(block_lanes(qn_ref, j) * kmax2)
            finish(j, sts[j], jnp.maximum(bound, sink))

    @pl.when(jnp.logical_not(fixed_ok))
    def _():
        for j in range(nsub):
            st = masked_scores(j)
            finish(j, st, jnp.maximum(jnp.max(st, axis=0, keepdims=True), sink))


def _band_bias():
    w = WINDOW
    r = np.arange(3 * w)[:, None] - w
    c = np.arange(w)[None, :]
    band = np.abs(r - c) <= w
    first = band & (r >= 0)
    last = band & (r < w)
    table = np.stack([first, band, last]).astype(np.float32)
    return (1.0 - table) * NEG_INF


def _attn_b_call(sink, qt, k, vt, qn, kn, *, nsub=8):
    b, _, _, s = qt.shape
    w = WINDOW
    nb = s // w
    span = nsub * w
    bias = jnp.asarray(_band_bias(), F32)
    left = lambda i: jnp.maximum(i * nsub - 1, 0)
    right = lambda i: jnp.minimum((i + 1) * nsub, nb - 1)

    return pl.pallas_call(
        functools.partial(_attn_b_kernel, nsub=nsub),
        out_shape=jax.ShapeDtypeStruct((b, s, Q_HEADS * HEAD_DIM), BF16),
        grid=(b, KV_HEADS, s // span),
        in_specs=[
            pl.BlockSpec(memory_space=pltpu.SMEM),
            pl.BlockSpec(memory_space=pltpu.SMEM),
            pl.BlockSpec((3, 3 * w, w), lambda bi, hk, i: (0, 0, 0)),
            pl.BlockSpec((1, GROUP, HEAD_DIM, span), lambda bi, hk, i: (bi, hk, 0, i)),
            pl.BlockSpec((1, 1, w, HEAD_DIM), lambda bi, hk, i: (bi, hk, left(i), 0)),
            pl.BlockSpec((1, 1, span, HEAD_DIM), lambda bi, hk, i: (bi, hk, i, 0)),
            pl.BlockSpec((1, 1, w, HEAD_DIM), lambda bi, hk, i: (bi, hk, right(i), 0)),
            pl.BlockSpec((1, 1, V_ROWS, w), lambda bi, hk, i: (bi, hk, 0, left(i))),
            pl.BlockSpec((1, 1, V_ROWS, span), lambda bi, hk, i: (bi, hk, 0, i)),
            pl.BlockSpec((1, 1, V_ROWS, w), lambda bi, hk, i: (bi, hk, 0, right(i))),
            pl.BlockSpec((1, GROUP, 1, span), lambda bi, hk, i: (bi, hk, 0, i)),
        ],
        out_specs=pl.BlockSpec((1, span, GROUP * HEAD_DIM), lambda bi, hk, i: (bi, i, hk)),
        compiler_params=pltpu.CompilerParams(
            dimension_semantics=("arbitrary", "arbitrary", "arbitrary"),
            vmem_limit_bytes=V7X_VMEM_LIMIT),
        name="attn_window",
    )(sink, _bound_stats(qn, kn), bias, qt, k, k, k, vt, vt, vt, qn)


def _merge_kernel(x_ref, mod_ref, ya_ref, yb_ref, sg_ref, wb_ref, wo_ref, o_ref):
    d = x_ref.shape[2]
    dot = functools.partial(jnp.dot, preferred_element_type=F32)
    ua = dot(ya_ref[0], wb_ref[0])
    ub = dot(yb_ref[0], wb_ref[1])
    sg = sg_ref[0]
    merged = sg[:, :d].astype(F32) * ua + sg[:, d:].astype(F32) * ub
    gate1 = mod_ref[0, 2:3, :]
    o_ref[0] = x_ref[0] + gate1 * dot(merged.astype(BF16), wo_ref[...])


def _merge_call(x, mod3, ya, yb, sg, wb, wo, *, tm=512):
    b, s, d = x.shape
    bw = ya.shape[2]
    tok = lambda bi, i: (bi, i, 0)
    return pl.pallas_call(
        _merge_kernel,
        out_shape=jax.ShapeDtypeStruct((b, s, d), F32),
        grid=(b, s // tm),
        in_specs=[
            pl.BlockSpec((1, tm, d), tok),
            pl.BlockSpec((1, 6, d), lambda bi, i: (bi, 0, 0)),
            pl.BlockSpec((1, tm, bw), tok),
            pl.BlockSpec((1, tm, bw), tok),
            pl.BlockSpec((1, tm, 2 * d), tok),
            pl.BlockSpec((2, bw, d), lambda bi, i: (0, 0, 0)),
            pl.BlockSpec((d, d), lambda bi, i: (0, 0)),
        ],
        out_specs=pl.BlockSpec((1, tm, d), tok),
        compiler_params=pltpu.CompilerParams(
            dimension_semantics=("arbitrary", "arbitrary"),
            vmem_limit_bytes=V7X_VMEM_LIMIT),
        name="merge_out_proj",
    )(x, mod3, ya, yb, sg, wb, wo)


def _mlp_kernel(x_ref, mod_ref, g2_ref, gf_ref, w1_ref, w2_ref, o_ref, *, n_chunks):
    x = x_ref[0]
    shift2 = mod_ref[0, 3:4, :]
    scale2 = mod_ref[0, 4:5, :]
    gate2 = mod_ref[0, 5:6, :]
    h2 = (_rms(x) * g2_ref[...] * (1.0 + scale2) + shift2).astype(BF16)
    ff = w1_ref.shape[1]
    cw = ff // n_chunks
    acc = jnp.zeros(x.shape, F32)
    for c in range(n_chunks):
        hid = jnp.dot(h2, w1_ref[:, c * cw:(c + 1) * cw], preferred_element_type=F32)
        hid = jnp.square(jnp.maximum(hid, 0.0)).astype(BF16)
        acc = acc + jnp.dot(hid, w2_ref[c * cw:(c + 1) * cw, :], preferred_element_type=F32)
    x2 = x + gate2 * acc
    o_ref[0] = _rms(x2) * gf_ref[...]


def _mlp_call(x, mod3, g2, gf, w1, w2, *, tm=512, n_chunks=4):
    b, s, d = x.shape
    ff = w1.shape[1]
    tok = lambda bi, i: (bi, i, 0)
    const2 = lambda bi, i: (0, 0)
    return pl.pallas_call(
        functools.partial(_mlp_kernel, n_chunks=n_chunks),
        out_shape=jax.ShapeDtypeStruct((b, s, d), F32),
        grid=(b, s // tm),
        in_specs=[
            pl.BlockSpec((1, tm, d), tok),
            pl.BlockSpec((1, 6, d), lambda bi, i: (bi, 0, 0)),
            pl.BlockSpec((1, d), const2),
            pl.BlockSpec((1, d), const2),
            pl.BlockSpec((d, ff), const2),
            pl.BlockSpec((ff, d), const2),
        ],
        out_specs=pl.BlockSpec((1, tm, d), tok),
        compiler_params=pltpu.CompilerParams(
            dimension_semantics=("arbitrary", "arbitrary"),
            vmem_limit_bytes=V7X_VMEM_LIMIT),
        name="mlp_final_norm",
    )(x, mod3, g2, gf, w1, w2)


def _rope_tables(s):
    def cos_sin(pos, dim, theta):
        inv = theta ** (-jnp.arange(0, dim, 2, dtype=F32) / dim)
        ang = pos.astype(F32)[:, None] * inv[None, :]
        return jnp.cos(ang).T, jnp.sin(ang).T

    t = jnp.arange(s, dtype=jnp.int32)
    cr, sr = cos_sin(t // GRID_W, HEAD_DIM // 2, AXIAL_THETA)
    cc, sc = cos_sin(t % GRID_W, HEAD_DIM // 2, AXIAL_THETA)
    c1, s1 = cos_sin(t, HEAD_DIM, ROPE_THETA)
    cos_a = jnp.concatenate([cr, cr, cc, cc], axis=0)
    sin_a = jnp.concatenate([-sr, sr, -sc, sc], axis=0)
    cos_b = jnp.concatenate([c1, c1], axis=0)
    sin_b = jnp.concatenate([-s1, s1], axis=0)
    return cos_a, sin_a, cos_b, sin_b


def kernel(x, c, w_ada, b_ada, norm1_g, w_in, q_norm_a, k_norm_a, sink_b, w_branch, w_out,
           norm2_g, w_mlp_in, w_mlp_out, final_g):
    b, s, d = x.shape
    assert w_ada.shape[0] == 1, "kernel supports DEPTH == 1"
    l = 0
    n_qkv = 2 * (Q_HEADS + 2 * KV_HEADS) * HEAD_DIM
    cos_a, sin_a, cos_b, sin_b = _rope_tables(s)
    c_pad = jnp.pad(c, ((0, 8 - b), (0, 0)))

    mod = _mod_call(c_pad, w_ada[l], b_ada[l][None, :])
    mod3 = mod[:b].reshape(b, 6, d)
    wt = w_in[l][:, :n_qkv].T.astype(BF16)
    wg = w_in[l][:, n_qkv:].astype(BF16)
    qa, ka, va, qna, kna, qb, kb, vb, qnb, knb, sg = _proj_call(
        x, mod3, norm1_g[l][None, :], wt, wg,
        q_norm_a[l][:, None], k_norm_a[l][:, None], cos_a, sin_a, cos_b, sin_b)
    ya = _attn_a_call(qa, ka, va, qna, kna)
    yb = _attn_b_call(sink_b[l], qb, kb, vb, qnb, knb)
    x1 = _merge_call(x, mod3, ya, yb, sg, w_branch[l].astype(BF16), w_out[l].astype(BF16))
    return _mlp_call(x1, mod3, norm2_g[l][None, :], final_g[None, :],
                     w_mlp_in[l].astype(BF16), w_mlp_out[l].astype(BF16))
```

```python
import functools
import math

import jax
import jax.numpy as jnp
import numpy as np
from jax import lax
from jax.experimental import pallas as pl
from jax.experimental.pallas import tpu as pltpu

F32 = jnp.float32
BF16 = jnp.bfloat16

HEAD_DIM = 64
Q_HEADS = 8
KV_HEADS = 2
GROUP = Q_HEADS // KV_HEADS
GRID_W = 64
WINDOW = 128
ROPE_THETA = 10000.0
AXIAL_THETA = 10000.0
NORM_EPS = 1e-6
NEG_INF = -1e30
LOG2E = math.log2(math.e)

V7X_BF16_SUBLANES = 16
V7X_VMEM_LIMIT = 56 * 1024 * 1024

V_ROWS = HEAD_DIM + V7X_BF16_SUBLANES

_BF16_SQUARE_MARGIN = (1.0 + 2.0 ** -8) ** 2


def _rms(x, eps=NORM_EPS):
    return x * lax.rsqrt(jnp.mean(x * x, axis=-1, keepdims=True) + eps)


def _mod_kernel(c_ref, w_ref, b_ref, o_ref):
    c = c_ref[...]
    a = c * (1.0 / (1.0 + jnp.exp(-c)))
    w = w_ref[...]
    a_hi = a.astype(BF16)
    a_lo = (a - a_hi.astype(F32)).astype(BF16)
    w_hi = w.astype(BF16)
    w_lo = (w - w_hi.astype(F32)).astype(BF16)
    dot = functools.partial(jnp.dot, preferred_element_type=F32)
    o_ref[...] = dot(a_hi, w_hi) + (dot(a_hi, w_lo) + dot(a_lo, w_hi)) + b_ref[...]


def _mod_call(c_pad, w_ada, b_ada, *, tn=1024):
    rows, d = c_pad.shape
    n = w_ada.shape[1]
    return pl.pallas_call(
        _mod_kernel,
        out_shape=jax.ShapeDtypeStruct((rows, n), F32),
        grid=(n // tn,),
        in_specs=[
            pl.BlockSpec((rows, d), lambda j: (0, 0)),
            pl.BlockSpec((d, tn), lambda j: (0, j)),
            pl.BlockSpec((1, tn), lambda j: (0, j)),
        ],
        out_specs=pl.BlockSpec((rows, tn), lambda j: (0, j)),
        compiler_params=pltpu.CompilerParams(
            dimension_semantics=("arbitrary",), vmem_limit_bytes=V7X_VMEM_LIMIT),
        name="adaln_mod",
    )(c_pad, w_ada, b_ada)


def _rope_t(x, cos, sin, half):
    parts = []
    for base in range(0, HEAD_DIM, 2 * half):
        parts.append(x[base + half:base + 2 * half])
        parts.append(x[base:base + half])
    x_rot = jnp.concatenate(parts, axis=0)
    return x * cos + x_rot * sin


def _proj_kernel(x_ref, mod_ref, g1_ref, wt_ref, wg_ref, gq_ref, gk_ref,
                 cos_a_ref, sin_a_ref, cos_b_ref, sin_b_ref,
                 qa_ref, ka_ref, va_ref, qna_ref, kna_ref,
                 qb_ref, kb_ref, vb_ref, qnb_ref, knb_ref, sg_ref, *, n_sub):
    shift1 = mod_ref[0, 0:1, :]
    scale1 = mod_ref[0, 1:2, :]
    g1 = g1_ref[...]
    gq, gk = gq_ref[...], gk_ref[...]
    q_scale = HEAD_DIM ** -0.5 * LOG2E
    ts = x_ref.shape[1] // n_sub
    ones = jnp.ones((V_ROWS - HEAD_DIM, ts), BF16)

    def normed(t, g):
        ms = jnp.sum(t * t, axis=0, keepdims=True) * (1.0 / HEAD_DIM)
        return t * lax.rsqrt(ms + NORM_EPS) * g

    def norm2_bound(t):
        return jnp.sum(t * t, axis=0, keepdims=True) * _BF16_SQUARE_MARGIN

    for sub in range(n_sub):
        tok = slice(sub * ts, (sub + 1) * ts)
        x = x_ref[0, tok, :]
        h = (_rms(x) * g1 * (1.0 + scale1) + shift1).astype(BF16)

        pt = lax.dot_general(wt_ref[...], h, (((1,), (1,)), ((), ())),
                             preferred_element_type=F32)
        cos_a, sin_a = cos_a_ref[:, tok], sin_a_ref[:, tok]
        cos_b, sin_b = cos_b_ref[:, tok], sin_b_ref[:, tok]

        def head(row):
            return pt[row:row + HEAD_DIM]

        row = 0
        for hq in range(Q_HEADS):
            t = _rope_t(normed(head(row), gq), cos_a, sin_a, HEAD_DIM // 4) * q_scale
            qa_ref[0, hq, :, tok] = t.astype(BF16)
            qna_ref[0, hq, :, tok] = norm2_bound(t)
            row += HEAD_DIM
        for hk in range(KV_HEADS):
            t = _rope_t(normed(head(row), gk), cos_a, sin_a, HEAD_DIM // 4)
            ka_ref[0, hk, tok, :] = t.T.astype(BF16)
            kna_ref[0, hk, :, tok] = norm2_bound(t)
            row += HEAD_DIM
        for hk in range(KV_HEADS):
            va_ref[0, hk, :, tok] = jnp.concatenate([head(row).astype(BF16), ones], axis=0)
            row += HEAD_DIM
        for hq in range(Q_HEADS):
            t = _rope_t(head(row), cos_b, sin_b, HEAD_DIM // 2) * q_scale
            qb_ref[0, hq, :, tok] = t.astype(BF16)
            qnb_ref[0, hq, :, tok] = norm2_bound(t)
            row += HEAD_DIM
        for hk in range(KV_HEADS):
            t = _rope_t(head(row), cos_b, sin_b, HEAD_DIM // 2)
            kb_ref[0, hk, tok, :] = t.T.astype(BF16)
            knb_ref[0, hk, :, tok] = norm2_bound(t)
            row += HEAD_DIM
        for hk in range(KV_HEADS):
            vb_ref[0, hk, :, tok] = jnp.concatenate([head(row).astype(BF16), ones], axis=0)
            row += HEAD_DIM

        g = jnp.dot(h, wg_ref[...], preferred_element_type=F32)
        sg_ref[0, tok, :] = (1.0 / (1.0 + jnp.exp(-g))).astype(BF16)


def _proj_call(x, mod3, g1, wt, wg, gq, gk, cos_a, sin_a, cos_b, sin_b, *, tm=1024, n_sub=4):
    b, s, d = x.shape
    n_qkv = wt.shape[0]
    n_g = wg.shape[1]
    const2 = lambda bi, i: (0, 0)
    tab = pl.BlockSpec((HEAD_DIM, tm), lambda bi, i: (0, i))
    q_shape = jax.ShapeDtypeStruct((b, Q_HEADS, HEAD_DIM, s), BF16)
    k_shape = jax.ShapeDtypeStruct((b, KV_HEADS, s, HEAD_DIM), BF16)
    v_shape = jax.ShapeDtypeStruct((b, KV_HEADS, V_ROWS, s), BF16)
    q_spec = pl.BlockSpec((1, Q_HEADS, HEAD_DIM, tm), lambda bi, i: (bi, 0, 0, i))
    k_spec = pl.BlockSpec((1, KV_HEADS, tm, HEAD_DIM), lambda bi, i: (bi, 0, i, 0))
    v_spec = pl.BlockSpec((1, KV_HEADS, V_ROWS, tm), lambda bi, i: (bi, 0, 0, i))
    qn_shape = jax.ShapeDtypeStruct((b, Q_HEADS, 1, s), F32)
    qn_spec = pl.BlockSpec((1, Q_HEADS, 1, tm), lambda bi, i: (bi, 0, 0, i))
    kn_shape = jax.ShapeDtypeStruct((b, KV_HEADS, 1, s), F32)
    kn_spec = pl.BlockSpec((1, KV_HEADS, 1, tm), lambda bi, i: (bi, 0, 0, i))
    mixer_shapes = (q_shape, k_shape, v_shape, qn_shape, kn_shape)
    mixer_specs = (q_spec, k_spec, v_spec, qn_spec, kn_spec)
    return pl.pallas_call(
        functools.partial(_proj_kernel, n_sub=n_sub),
        out_shape=mixer_shapes + mixer_shapes + (jax.ShapeDtypeStruct((b, s, n_g), BF16),),
        grid=(b, s // tm),
        in_specs=[
            pl.BlockSpec((1, tm, d), lambda bi, i: (bi, i, 0)),
            pl.BlockSpec((1, 6, d), lambda bi, i: (bi, 0, 0)),
            pl.BlockSpec((1, d), const2),
            pl.BlockSpec((n_qkv, d), const2),
            pl.BlockSpec((d, n_g), const2),
            pl.BlockSpec((HEAD_DIM, 1), const2),
            pl.BlockSpec((HEAD_DIM, 1), const2),
            tab, tab, tab, tab,
        ],
        out_specs=mixer_specs + mixer_specs + (
            pl.BlockSpec((1, tm, n_g), lambda bi, i: (bi, i, 0)),),
        compiler_params=pltpu.CompilerParams(
            dimension_semantics=("arbitrary", "arbitrary"),
            vmem_limit_bytes=V7X_VMEM_LIMIT),
        name="in_proj",
    )(x, mod3, g1, wt, wg, gq, gk, cos_a, sin_a, cos_b, sin_b)


_MAX_FIXED_STABILISER = 50.0


def _attn_a_finish(acc, o_ref, tq):
    o = acc[:HEAD_DIM] * (1.0 / acc[HEAD_DIM:HEAD_DIM + 1])
    o_ref[0] = jnp.concatenate(
        [o[:, hq * tq:(hq + 1) * tq].T for hq in range(GROUP)], axis=1).astype(BF16)


def _bound_stats(qn, kn):
    b = qn.shape[0]
    kmax2 = jnp.max(kn, axis=(2, 3))
    qmax2 = jnp.max(qn.reshape(b, KV_HEADS, -1), axis=2)
    return jnp.stack([kmax2, qmax2 * kmax2], axis=-1).reshape(-1)


def _read_bound_stats(stats_ref):
    base = (pl.program_id(0) * KV_HEADS + pl.program_id(1)) * 2
    return stats_ref[base], stats_ref[base + 1] <= _MAX_FIXED_STABILISER ** 2


def _attn_a_kernel(stats_ref, q_ref, k_ref, v_ref, qn_ref, o_ref, *, tk, unroll):
    tq = q_ref.shape[3]
    n = GROUP * tq
    n_chunks = k_ref.shape[2] // tk
    q = jnp.concatenate([q_ref[0, hq] for hq in range(GROUP)], axis=1)
    kmax2, fixed_ok = _read_bound_stats(stats_ref)

    def chunk(c):
        start = pl.multiple_of(c * tk, tk)
        return k_ref[0, 0, pl.ds(start, tk), :], v_ref[0, 0, :, pl.ds(start, tk)]

    @pl.when(fixed_ok)
    def _():
        qn2 = jnp.concatenate([qn_ref[0, hq] for hq in range(GROUP)], axis=1)
        bound = jnp.sqrt(qn2 * kmax2)

        def scores(c):
            return jnp.dot(chunk(c)[0], q, preferred_element_type=F32)

        def consume(st, c, acc):
            p = jnp.exp2(st - bound).astype(BF16)
            return acc + jnp.dot(chunk(c)[1], p, preferred_element_type=F32)

        def body(c, carry):
            st, acc = carry
            st_next = scores(c + 1)
            return st_next, consume(st, c, acc)

        init = (scores(0), jnp.zeros((V_ROWS, n), F32))
        st, acc = lax.fori_loop(0, n_chunks - 1, body, init, unroll=unroll)
        _attn_a_finish(consume(st, n_chunks - 1, acc), o_ref, tq)

    @pl.when(jnp.logical_not(fixed_ok))
    def _():
        def body(c, carry):
            m, acc = carry
            kc, vc = chunk(c)
            st = jnp.dot(kc, q, preferred_element_type=F32)
            m_new = jnp.maximum(m, jnp.max(st, axis=0, keepdims=True))
            p = jnp.exp2(st - m_new).astype(BF16)
            acc = acc * jnp.exp2(m - m_new) + jnp.dot(vc, p, preferred_element_type=F32)
            return m_new, acc

        init = (jnp.full((1, n), NEG_INF, F32), jnp.zeros((V_ROWS, n), F32))
        _, acc = lax.fori_loop(0, n_chunks, body, init)
        _attn_a_finish(acc, o_ref, tq)


def _attn_a_call(qt, k, vt, qn, kn, *, tq=256, tk=256, unroll=True):
    b, _, _, s = qt.shape
    return pl.pallas_call(
        functools.partial(_attn_a_kernel, tk=tk, unroll=unroll),
        out_shape=jax.ShapeDtypeStruct((b, s, Q_HEADS * HEAD_DIM), BF16),
        grid=(b, KV_HEADS, s // tq),
        in_specs=[
            pl.BlockSpec(memory_space=pltpu.SMEM),
            pl.BlockSpec((1, GROUP, HEAD_DIM, tq), lambda bi, hk, i: (bi, hk, 0, i)),
            pl.BlockSpec((1, 1, s, HEAD_DIM), lambda bi, hk, i: (bi, hk, 0, 0)),
            pl.BlockSpec((1, 1, V_ROWS, s), lambda bi, hk, i: (bi, hk, 0, 0)),
            pl.BlockSpec((1, GROUP, 1, tq), lambda bi, hk, i: (bi, hk, 0, i)),
        ],
        out_specs=pl.BlockSpec((1, tq, GROUP * HEAD_DIM), lambda bi, hk, i: (bi, i, hk)),
        compiler_params=pltpu.CompilerParams(
            dimension_semantics=("arbitrary", "arbitrary", "arbitrary"),
            vmem_limit_bytes=V7X_VMEM_LIMIT),
        name="attn_global",
    )(_bound_stats(qn, kn), qt, k, vt, qn)


def _attn_b_kernel(sink_ref, stats_ref, bias_ref, q_ref, kl_ref, kc_ref, kr_ref,
                   vl_ref, vc_ref, vr_ref, qn_ref, o_ref, *, nsub):
    hk = pl.program_id(1)
    i = pl.program_id(2)
    last_i = pl.num_programs(2) - 1
    w = kl_ref.shape[2]
    k = jnp.concatenate([kl_ref[0, 0], kc_ref[0, 0], kr_ref[0, 0]], axis=0)
    v = jnp.concatenate([vl_ref[0, 0], vc_ref[0, 0], vr_ref[0, 0]], axis=1)
    sink = jnp.concatenate(
        [jnp.full((1, w), sink_ref[hk * GROUP + hq] * LOG2E, F32) for hq in range(GROUP)],
        axis=1)
    band = bias_ref[1]
    kmax2, fixed_ok = _read_bound_stats(stats_ref)

    def block_lanes(ref, j):
        return jnp.concatenate([ref[0, hq, :, j * w:(j + 1) * w] for hq in range(GROUP)], axis=1)

    def masked_scores(j):
        bias = band
        if j == 0:
            bias = jnp.where(i == 0, bias_ref[0], band)
        if j == nsub - 1:
            bias = jnp.where(i == last_i, bias_ref[2], bias)
        st = jnp.dot(k[j * w:(j + 3) * w], block_lanes(q_ref, j),
                     preferred_element_type=F32)
        return st + jnp.concatenate([bias] * GROUP, axis=1)

    def finish(j, st, m):
        p = jnp.exp2(st - m).astype(BF16)
        acc = jnp.dot(v[:, j * w:(j + 3) * w], p, preferred_element_type=F32)
        denom = acc[HEAD_DIM:HEAD_DIM + 1] + jnp.exp2(sink - m)
        o = acc[:HEAD_DIM] * (1.0 / denom)
        o_ref[0, j * w:(j + 1) * w, :] = jnp.concatenate(
            [o[:, hq * w:(hq + 1) * w].T for hq in range(GROUP)], axis=1).astype(BF16)

    @pl.when(fixed_ok)
    def _():
        ahead = 2
        sts = [masked_scores(j) for j in range(min(ahead, nsub))]
        for j in range(nsub):
            if j + ahead < nsub:
                sts.append(masked_scores(j + ahead))
            bound = jnp.sqrt(block_lanes(qn_ref, j) * kmax2)
            finish(j, sts[j], jnp.maximum(bound, sink))

    @pl.when(jnp.logical_not(fixed_ok))
    def _():
        for j in range(nsub):
            st = masked_scores(j)
            finish(j, st, jnp.maximum(jnp.max(st, axis=0, keepdims=True), sink))


def _band_bias():
    w = WINDOW
    r = np.arange(3 * w)[:, None] - w
    c = np.arange(w)[None, :]
    band = np.abs(r - c) <= w
    first = band & (r >= 0)
    last = band & (r < w)
    table = np.stack([first, band, last]).astype(np.float32)
    return (1.0 - table) * NEG_INF


def _attn_b_call(sink, qt, k, vt, qn, kn, *, nsub=8):
    b, _, _, s = qt.shape
    w = WINDOW
    nb = s // w
    span = nsub * w
    bias = jnp.asarray(_band_bias(), F32)
    left = lambda i: jnp.maximum(i * nsub - 1, 0)
    right = lambda i: jnp.minimum((i + 1) * nsub, nb - 1)

    return pl.pallas_call(
        functools.partial(_attn_b_kernel, nsub=nsub),
        out_shape=jax.ShapeDtypeStruct((b, s, Q_HEADS * HEAD_DIM), BF16),
        grid=(b, KV_HEADS, s // span),
        in_specs=[
            pl.BlockSpec(memory_space=pltpu.SMEM),
            pl.BlockSpec(memory_space=pltpu.SMEM),
            pl.BlockSpec((3, 3 * w, w), lambda bi, hk, i: (0, 0, 0)),
            pl.BlockSpec((1, GROUP, HEAD_DIM, span), lambda bi, hk, i: (bi, hk, 0, i)),
            pl.BlockSpec((1, 1, w, HEAD_DIM), lambda bi, hk, i: (bi, hk, left(i), 0)),
            pl.BlockSpec((1, 1, span, HEAD_DIM), lambda bi, hk, i: (bi, hk, i, 0)),
            pl.BlockSpec((1, 1, w, HEAD_DIM), lambda bi, hk, i: (bi, hk, right(i), 0)),
            pl.BlockSpec((1, 1, V_ROWS, w), lambda bi, hk, i: (bi, hk, 0, left(i))),
            pl.BlockSpec((1, 1, V_ROWS, span), lambda bi, hk, i: (bi, hk, 0, i)),
            pl.BlockSpec((1, 1, V_ROWS, w), lambda bi, hk, i: (bi, hk, 0, right(i))),
            pl.BlockSpec((1, GROUP, 1, span), lambda bi, hk, i: (bi, hk, 0, i)),
        ],
        out_specs=pl.BlockSpec((1, span, GROUP * HEAD_DIM), lambda bi, hk, i: (bi, i, hk)),
        compiler_params=pltpu.CompilerParams(
            dimension_semantics=("arbitrary", "arbitrary", "arbitrary"),
            vmem_limit_bytes=V7X_VMEM_LIMIT),
        name="attn_window",
    )(sink, _bound_stats(qn, kn), bias, qt, k, k, k, vt, vt, vt, qn)


def _channel_kernel(x_ref, mod_ref, ya_ref, yb_ref, sg_ref, wb_ref, wo_ref,
                    g2_ref, gf_ref, w1_ref, w2_ref, o_ref, *, n_sub, n_chunks):
    d = x_ref.shape[2]
    ts = x_ref.shape[1] // n_sub
    cw = w1_ref.shape[1] // n_chunks
    dot = functools.partial(jnp.dot, preferred_element_type=F32)
    gate1 = mod_ref[0, 2:3, :]
    shift2 = mod_ref[0, 3:4, :]
    scale2 = mod_ref[0, 4:5, :]
    gate2 = mod_ref[0, 5:6, :]
    for sub in range(n_sub):
        tok = slice(sub * ts, (sub + 1) * ts)
        ua = dot(ya_ref[0, tok, :], wb_ref[0])
        ub = dot(yb_ref[0, tok, :], wb_ref[1])
        sg = sg_ref[0, tok, :]
        merged = sg[:, :d].astype(F32) * ua + sg[:, d:].astype(F32) * ub
        x1 = x_ref[0, tok, :] + gate1 * dot(merged.astype(BF16), wo_ref[...])

        h2 = (_rms(x1) * g2_ref[...] * (1.0 + scale2) + shift2).astype(BF16)
        acc = jnp.zeros((ts, d), F32)
        for c in range(n_chunks):
            hid = dot(h2, w1_ref[:, c * cw:(c + 1) * cw])
            hid = jnp.square(jnp.maximum(hid, 0.0)).astype(BF16)
            acc = acc + dot(hid, w2_ref[c * cw:(c + 1) * cw, :])
        x2 = x1 + gate2 * acc
        o_ref[0, tok, :] = _rms(x2) * gf_ref[...]


def _channel_call(x, mod3, ya, yb, sg, wb, wo, g2, gf, w1, w2, *, tm=512, n_sub=2, n_chunks=4):
    b, s, d = x.shape
    bw = ya.shape[2]
    ff = w1.shape[1]
    tok = lambda bi, i: (bi, i, 0)
    const2 = lambda bi, i: (0, 0)
    once = pl.Buffered(1)
    return pl.pallas_call(
        functools.partial(_channel_kernel, n_sub=n_sub, n_chunks=n_chunks),
        out_shape=jax.ShapeDtypeStruct((b, s, d), F32),
        grid=(b, s // tm),
        in_specs=[
            pl.BlockSpec((1, tm, d), tok),
            pl.BlockSpec((1, 6, d), lambda bi, i: (bi, 0, 0)),
            pl.BlockSpec((1, tm, bw), tok),
            pl.BlockSpec((1, tm, bw), tok),
            pl.BlockSpec((1, tm, 2 * d), tok),
            pl.BlockSpec((2, bw, d), lambda bi, i: (0, 0, 0), pipeline_mode=once),
            pl.BlockSpec((d, d), const2, pipeline_mode=once),
            pl.BlockSpec((1, d), const2),
            pl.BlockSpec((1, d), const2),
            pl.BlockSpec((d, ff), const2, pipeline_mode=once),
            pl.BlockSpec((ff, d), const2, pipeline_mode=once),
        ],
        out_specs=pl.BlockSpec((1, tm, d), tok),
        compiler_params=pltpu.CompilerParams(
            dimension_semantics=("arbitrary", "arbitrary"),
            vmem_limit_bytes=V7X_VMEM_LIMIT),
        name="merge_mlp_final_norm",
    )(x, mod3, ya, yb, sg, wb, wo, g2, gf, w1, w2)


def _rope_tables(s):
    def cos_sin(pos, dim, theta):
        inv = theta ** (-jnp.arange(0, dim, 2, dtype=F32) / dim)
        ang = pos.astype(F32)[:, None] * inv[None, :]
        return jnp.cos(ang).T, jnp.sin(ang).T

    t = jnp.arange(s, dtype=jnp.int32)
    cr, sr = cos_sin(t // GRID_W, HEAD_DIM // 2, AXIAL_THETA)
    cc, sc = cos_sin(t % GRID_W, HEAD_DIM // 2, AXIAL_THETA)
    c1, s1 = cos_sin(t, HEAD_DIM, ROPE_THETA)
    cos_a = jnp.concatenate([cr, cr, cc, cc], axis=0)
    sin_a = jnp.concatenate([-sr, sr, -sc, sc], axis=0)
    cos_b = jnp.concatenate([c1, c1], axis=0)
    sin_b = jnp.concatenate([-s1, s1], axis=0)
    return cos_a, sin_a, cos_b, sin_b


def kernel(x, c, w_ada, b_ada, norm1_g, w_in, q_norm_a, k_norm_a, sink_b, w_branch, w_out,
           norm2_g, w_mlp_in, w_mlp_out, final_g):
    b, s, d = x.shape
    assert w_ada.shape[0] == 1, "kernel supports DEPTH == 1"
    l = 0
    n_qkv = 2 * (Q_HEADS + 2 * KV_HEADS) * HEAD_DIM
    cos_a, sin_a, cos_b, sin_b = _rope_tables(s)
    c_pad = jnp.pad(c, ((0, 8 - b), (0, 0)))

    mod = _mod_call(c_pad, w_ada[l], b_ada[l][None, :])
    mod3 = mod[:b].reshape(b, 6, d)
    wt = w_in[l][:, :n_qkv].T.astype(BF16)
    wg = w_in[l][:, n_qkv:].astype(BF16)
    qa, ka, va, qna, kna, qb, kb, vb, qnb, knb, sg = _proj_call(
        x, mod3, norm1_g[l][None, :], wt, wg,
        q_norm_a[l][:, None], k_norm_a[l][:, None], cos_a, sin_a, cos_b, sin_b)
    ya = _attn_a_call(qa, ka, va, qna, kna)
    yb = _attn_b_call(sink_b[l], qb, kb, vb, qnb, knb)
    return _channel_call(x, mod3, ya, yb, sg, w_branch[l].astype(BF16), w_out[l].astype(BF16),
                         norm2_g[l][None, :], final_g[None, :],
                         w_mlp_in[l].astype(BF16), w_mlp_out[l].astype(BF16))
```

```python
import functools
import math

import jax
import jax.numpy as jnp
import numpy as np
from jax import lax
from jax.experimental import pallas as pl
from jax.experimental.pallas import tpu as pltpu

F32 = jnp.float32
BF16 = jnp.bfloat16

HEAD_DIM = 64
Q_HEADS = 8
KV_HEADS = 2
GROUP = Q_HEADS // KV_HEADS
GRID_W = 64
WINDOW = 128
ROPE_THETA = 10000.0
AXIAL_THETA = 10000.0
NORM_EPS = 1e-6
NEG_INF = -1e30
LOG2E = math.log2(math.e)

V7X_BF16_SUBLANES = 16
V7X_VMEM_LIMIT = 56 * 1024 * 1024

V_ROWS = HEAD_DIM + V7X_BF16_SUBLANES

_BF16_SQUARE_MARGIN = (1.0 + 2.0 ** -8) ** 2


def _rms(x, eps=NORM_EPS):
    return x * lax.rsqrt(jnp.mean(x * x, axis=-1, keepdims=True) + eps)


def _mod_kernel(c_ref, w_ref, b_ref, o_ref):
    c = c_ref[...]
    a = c * (1.0 / (1.0 + jnp.exp(-c)))
    w = w_ref[...]
    a_hi = a.astype(BF16)
    a_lo = (a - a_hi.astype(F32)).astype(BF16)
    w_hi = w.astype(BF16)
    w_lo = (w - w_hi.astype(F32)).astype(BF16)
    dot = functools.partial(jnp.dot, preferred_element_type=F32)
    o_ref[...] = dot(a_hi, w_hi) + (dot(a_hi, w_lo) + dot(a_lo, w_hi)) + b_ref[...]


def _mod_call(c_pad, w_ada, b_ada, *, tn=1024):
    rows, d = c_pad.shape
    n = w_ada.shape[1]
    return pl.pallas_call(
        _mod_kernel,
        out_shape=jax.ShapeDtypeStruct((rows, n), F32),
        grid=(n // tn,),
        in_specs=[
            pl.BlockSpec((rows, d), lambda j: (0, 0)),
            pl.BlockSpec((d, tn), lambda j: (0, j)),
            pl.BlockSpec((1, tn), lambda j: (0, j)),
        ],
        out_specs=pl.BlockSpec((rows, tn), lambda j: (0, j)),
        compiler_params=pltpu.CompilerParams(
            dimension_semantics=("arbitrary",), vmem_limit_bytes=V7X_VMEM_LIMIT),
        name="adaln_mod",
    )(c_pad, w_ada, b_ada)


def _rope_t(x, cos, sin, half):
    parts = []
    for base in range(0, HEAD_DIM, 2 * half):
        parts.append(x[base + half:base + 2 * half])
        parts.append(x[base:base + half])
    x_rot = jnp.concatenate(parts, axis=0)
    return x * cos + x_rot * sin


def _proj_kernel(x_ref, mod_ref, g1_ref, wt_ref, wg_ref, gq_ref, gk_ref,
                 cos_a_ref, sin_a_ref, cos_b_ref, sin_b_ref,
                 qa_ref, ka_ref, va_ref, qna_ref, kna_ref,
                 qb_ref, kb_ref, vb_ref, qnb_ref, knb_ref, sg_ref, *, n_sub):
    shift1 = mod_ref[0, 0:1, :]
    scale1 = mod_ref[0, 1:2, :]
    g1 = g1_ref[...]
    gq, gk = gq_ref[...], gk_ref[...]
    q_scale = HEAD_DIM ** -0.5 * LOG2E
    ts = x_ref.shape[1] // n_sub
    ones = jnp.ones((V_ROWS - HEAD_DIM, ts), BF16)

    def normed(t, g):
        ms = jnp.sum(t * t, axis=0, keepdims=True) * (1.0 / HEAD_DIM)
        return t * lax.rsqrt(ms + NORM_EPS) * g

    def norm2_bound(t):
        return jnp.sum(t * t, axis=0, keepdims=True) * _BF16_SQUARE_MARGIN

    for sub in range(n_sub):
        tok = slice(sub * ts, (sub + 1) * ts)
        x = x_ref[0, tok, :]
        h = (_rms(x) * g1 * (1.0 + scale1) + shift1).astype(BF16)

        pt = lax.dot_general(wt_ref[...], h, (((1,), (1,)), ((), ())),
                             preferred_element_type=F32)
        cos_a, sin_a = cos_a_ref[:, tok], sin_a_ref[:, tok]
        cos_b, sin_b = cos_b_ref[:, tok], sin_b_ref[:, tok]

        def head(row):
            return pt[row:row + HEAD_DIM]

        row = 0
        for hq in range(Q_HEADS):
            t = _rope_t(normed(head(row), gq), cos_a, sin_a, HEAD_DIM // 4) * q_scale
            qa_ref[0, hq, :, tok] = t.astype(BF16)
            qna_ref[0, hq, :, tok] = norm2_bound(t)
            row += HEAD_DIM
        for hk in range(KV_HEADS):
            t = _rope_t(normed(head(row), gk), cos_a, sin_a, HEAD_DIM // 4)
            ka_ref[0, hk, tok, :] = t.T.astype(BF16)
            kna_ref[0, hk, :, tok] = norm2_bound(t)
            row += HEAD_DIM
        for hk in range(KV_HEADS):
            va_ref[0, hk, :, tok] = jnp.concatenate([head(row).astype(BF16), ones], axis=0)
            row += HEAD_DIM
        for hq in range(Q_HEADS):
            t = _rope_t(head(row), cos_b, sin_b, HEAD_DIM // 2) * q_scale
            qb_ref[0, hq, :, tok] = t.astype(BF16)
            qnb_ref[0, hq, :, tok] = norm2_bound(t)
            row += HEAD_DIM
        for hk in range(KV_HEADS):
            t = _rope_t(head(row), cos_b, sin_b, HEAD_DIM // 2)
            kb_ref[0, hk, tok, :] = t.T.astype(BF16)
            knb_ref[0, hk, :, tok] = norm2_bound(t)
            row += HEAD_DIM
        for hk in range(KV_HEADS):
            vb_ref[0, hk, :, tok] = jnp.concatenate([head(row).astype(BF16), ones], axis=0)
            row += HEAD_DIM

        g = jnp.dot(h, wg_ref[...], preferred_element_type=F32)
        sg_ref[0, tok, :] = (1.0 / (1.0 + jnp.exp(-g))).astype(BF16)


def _proj_call(x, mod3, g1, wt, wg, gq, gk, cos_a, sin_a, cos_b, sin_b, *, tm=1024, n_sub=4):
    b, s, d = x.shape
    n_qkv = wt.shape[0]
    n_g = wg.shape[1]
    const2 = lambda bi, i: (0, 0)
    tab = pl.BlockSpec((HEAD_DIM, tm), lambda bi, i: (0, i))
    q_shape = jax.ShapeDtypeStruct((b, Q_HEADS, HEAD_DIM, s), BF16)
    k_shape = jax.ShapeDtypeStruct((b, KV_HEADS, s, HEAD_DIM), BF16)
    v_shape = jax.ShapeDtypeStruct((b, KV_HEADS, V_ROWS, s), BF16)
    q_spec = pl.BlockSpec((1, Q_HEADS, HEAD_DIM, tm), lambda bi, i: (bi, 0, 0, i))
    k_spec = pl.BlockSpec((1, KV_HEADS, tm, HEAD_DIM), lambda bi, i: (bi, 0, i, 0))
    v_spec = pl.BlockSpec((1, KV_HEADS, V_ROWS, tm), lambda bi, i: (bi, 0, 0, i))
    qn_shape = jax.ShapeDtypeStruct((b, Q_HEADS, 1, s), F32)
    qn_spec = pl.BlockSpec((1, Q_HEADS, 1, tm), lambda bi, i: (bi, 0, 0, i))
    kn_shape = jax.ShapeDtypeStruct((b, KV_HEADS, 1, s), F32)
    kn_spec = pl.BlockSpec((1, KV_HEADS, 1, tm), lambda bi, i: (bi, 0, 0, i))
    mixer_shapes = (q_shape, k_shape, v_shape, qn_shape, kn_shape)
    mixer_specs = (q_spec, k_spec, v_spec, qn_spec, kn_spec)
    return pl.pallas_call(
        functools.partial(_proj_kernel, n_sub=n_sub),
        out_shape=mixer_shapes + mixer_shapes + (jax.ShapeDtypeStruct((b, s, n_g), BF16),),
        grid=(b, s // tm),
        in_specs=[
            pl.BlockSpec((1, tm, d), lambda bi, i: (bi, i, 0)),
            pl.BlockSpec((1, 6, d), lambda bi, i: (bi, 0, 0)),
            pl.BlockSpec((1, d), const2),
            pl.BlockSpec((n_qkv, d), const2),
            pl.BlockSpec((d, n_g), const2),
            pl.BlockSpec((HEAD_DIM, 1), const2),
            pl.BlockSpec((HEAD_DIM, 1), const2),
            tab, tab, tab, tab,
        ],
        out_specs=mixer_specs + mixer_specs + (
            pl.BlockSpec((1, tm, n_g), lambda bi, i: (bi, i, 0)),),
        compiler_params=pltpu.CompilerParams(
            dimension_semantics=("arbitrary", "arbitrary"),
            vmem_limit_bytes=V7X_VMEM_LIMIT),
        name="in_proj",
    )(x, mod3, g1, wt, wg, gq, gk, cos_a, sin_a, cos_b, sin_b)


_MAX_FIXED_STABILISER = 50.0


def _attn_a_finish(acc, o_ref, tq):
    o = acc[:HEAD_DIM] * (1.0 / acc[HEAD_DIM:HEAD_DIM + 1])
    o_ref[0] = jnp.concatenate(
        [o[:, hq * tq:(hq + 1) * tq].T for hq in range(GROUP)], axis=1).astype(BF16)


def _bound_stats(qn, kn):
    b = qn.shape[0]
    kmax2 = jnp.max(kn, axis=(2, 3))
    qmax2 = jnp.max(qn.reshape(b, KV_HEADS, -1), axis=2)
    return jnp.stack([kmax2, qmax2 * kmax2], axis=-1).reshape(-1)


def _read_bound_stats(stats_ref):
    base = (pl.program_id(0) * KV_HEADS + pl.program_id(1)) * 2
    return stats_ref[base], stats_ref[base + 1] <= _MAX_FIXED_STABILISER ** 2


def _attn_a_kernel(stats_ref, q_ref, k_ref, v_ref, qn_ref, o_ref, *, tk, unroll):
    tq = q_ref.shape[3]
    n = GROUP * tq
    n_chunks = k_ref.shape[2] // tk
    q = jnp.concatenate([q_ref[0, hq] for hq in range(GROUP)], axis=1)
    kmax2, fixed_ok = _read_bound_stats(stats_ref)

    def chunk(c):
        start = pl.multiple_of(c * tk, tk)
        return k_ref[0, 0, pl.ds(start, tk), :], v_ref[0, 0, :, pl.ds(start, tk)]

    @pl.when(fixed_ok)
    def _():
        qn2 = jnp.concatenate([qn_ref[0, hq] for hq in range(GROUP)], axis=1)
        bound = jnp.sqrt(qn2 * kmax2)

        def scores(c):
            return jnp.dot(chunk(c)[0], q, preferred_element_type=F32)

        def consume(st, c, acc):
            p = jnp.exp2(st - bound).astype(BF16)
            return acc + jnp.dot(chunk(c)[1], p, preferred_element_type=F32)

        def body(c, carry):
            st, acc = carry
            st_next = scores(c + 1)
            return st_next, consume(st, c, acc)

        init = (scores(0), jnp.zeros((V_ROWS, n), F32))
        st, acc = lax.fori_loop(0, n_chunks - 1, body, init, unroll=unroll)
        _attn_a_finish(consume(st, n_chunks - 1, acc), o_ref, tq)

    @pl.when(jnp.logical_not(fixed_ok))
    def _():
        def body(c, carry):
            m, acc = carry
            kc, vc = chunk(c)
            st = jnp.dot(kc, q, preferred_element_type=F32)
            m_new = jnp.maximum(m, jnp.max(st, axis=0, keepdims=True))
            p = jnp.exp2(st - m_new).astype(BF16)
            acc = acc * jnp.exp2(m - m_new) + jnp.dot(vc, p, preferred_element_type=F32)
            return m_new, acc

        init = (jnp.full((1, n), NEG_INF, F32), jnp.zeros((V_ROWS, n), F32))
        _, acc = lax.fori_loop(0, n_chunks, body, init)
        _attn_a_finish(acc, o_ref, tq)


def _attn_a_call(qt, k, vt, qn, kn, *, tq=128, tk=256, unroll=True):
    b, _, _, s = qt.shape
    return pl.pallas_call(
        functools.partial(_attn_a_kernel, tk=tk, unroll=unroll),
        out_shape=jax.ShapeDtypeStruct((b, s, Q_HEADS * HEAD_DIM), BF16),
        grid=(b, KV_HEADS, s // tq),
        in_specs=[
            pl.BlockSpec(memory_space=pltpu.SMEM),
            pl.BlockSpec((1, GROUP, HEAD_DIM, tq), lambda bi, hk, i: (bi, hk, 0, i)),
            pl.BlockSpec((1, 1, s, HEAD_DIM), lambda bi, hk, i: (bi, hk, 0, 0)),
            pl.BlockSpec((1, 1, V_ROWS, s), lambda bi, hk, i: (bi, hk, 0, 0)),
            pl.BlockSpec((1, GROUP, 1, tq), lambda bi, hk, i: (bi, hk, 0, i)),
        ],
        out_specs=pl.BlockSpec((1, tq, GROUP * HEAD_DIM), lambda bi, hk, i: (bi, i, hk)),
        compiler_params=pltpu.CompilerParams(
            dimension_semantics=("arbitrary", "arbitrary", "arbitrary"),
            vmem_limit_bytes=V7X_VMEM_LIMIT),
        name="attn_global",
    )(_bound_stats(qn, kn), qt, k, vt, qn)


def _attn_b_kernel(sink_ref, stats_ref, bias_ref, q_ref, kl_ref, kc_ref, kr_ref,
                   vl_ref, vc_ref, vr_ref, qn_ref, o_ref, *, nsub):
    hk = pl.program_id(1)
    i = pl.program_id(2)
    last_i = pl.num_programs(2) - 1
    w = kl_ref.shape[2]
    k = jnp.concatenate([kl_ref[0, 0], kc_ref[0, 0], kr_ref[0, 0]], axis=0)
    v = jnp.concatenate([vl_ref[0, 0], vc_ref[0, 0], vr_ref[0, 0]], axis=1)
    sink = jnp.concatenate(
        [jnp.full((1, w), sink_ref[hk * GROUP + hq] * LOG2E, F32) for hq in range(GROUP)],
        axis=1)
    band = bias_ref[1]
    kmax2, fixed_ok = _read_bound_stats(stats_ref)

    def block_lanes(ref, j):
        return jnp.concatenate([ref[0, hq, :, j * w:(j + 1) * w] for hq in range(GROUP)], axis=1)

    def masked_scores(j):
        bias = band
        if j == 0:
            bias = jnp.where(i == 0, bias_ref[0], band)
        if j == nsub - 1:
            bias = jnp.where(i == last_i, bias_ref[2], bias)
        st = jnp.dot(k[j * w:(j + 3) * w], block_lanes(q_ref, j),
                     preferred_element_type=F32)
        return st + jnp.concatenate([bias] * GROUP, axis=1)

    def finish(j, st, m):
        p = jnp.exp2(st - m).astype(BF16)
        acc = jnp.dot(v[:, j * w:(j + 3) * w], p, preferred_element_type=F32)
        denom = acc[HEAD_DIM:HEAD_DIM + 1] + jnp.exp2(sink - m)
        o = acc[:HEAD_DIM] * (1.0 / denom)
        o_ref[0, j * w:(j + 1) * w, :] = jnp.concatenate(
            [o[:, hq * w:(hq + 1) * w].T for hq in range(GROUP)], axis=1).astype(BF16)

    @pl.when(fixed_ok)
    def _():
        ahead = 2
        sts = [masked_scores(j) for j in range(min(ahead, nsub))]
        for j in range(nsub):
            if j + ahead < nsub:
                sts.append(masked_scores(j + ahead))
            bound = jnp.sqrt(block_lanes(qn_ref, j) * kmax2)
            finish(j, sts[j], jnp.maximum(bound, sink))

    @pl.when(jnp.logical_not(fixed_ok))
    def _():
        for j in range(nsub):
            st = masked_scores(j)
            finish(j, st, jnp.maximum(jnp.max(st, axis=0, keepdims=True), sink))


def _band_bias():
    w = WINDOW
    r = np.arange(3 * w)[:, None] - w
    c = np.arange(w)[None, :]
    band = np.abs(r - c) <= w
    first = band & (r >= 0)
    last = band & (r < w)
    table = np.stack([first, band, last]).astype(np.float32)
    return (1.0 - table) * NEG_INF


def _attn_b_call(sink, qt, k, vt, qn, kn, *, nsub=8):
    b, _, _, s = qt.shape
    w = WINDOW
    nb = s // w
    span = nsub * w
    bias = jnp.asarray(_band_bias(), F32)
    left = lambda i: jnp.maximum(i * nsub - 1, 0)
    right = lambda i: jnp.minimum((i + 1) * nsub, nb - 1)

    return pl.pallas_call(
        functools.partial(_attn_b_kernel, nsub=nsub),
        out_shape=jax.ShapeDtypeStruct((b, s, Q_HEADS * HEAD_DIM), BF16),
        grid=(b, KV_HEADS, s // span),
        in_specs=[
            pl.BlockSpec(memory_space=pltpu.SMEM),
            pl.BlockSpec(memory_space=pltpu.SMEM),
            pl.BlockSpec((3, 3 * w, w), lambda bi, hk, i: (0, 0, 0)),
            pl.BlockSpec((1, GROUP, HEAD_DIM, span), lambda bi, hk, i: (bi, hk, 0, i)),
            pl.BlockSpec((1, 1, w, HEAD_DIM), lambda bi, hk, i: (bi, hk, left(i), 0)),
            pl.BlockSpec((1, 1, span, HEAD_DIM), lambda bi, hk, i: (bi, hk, i, 0)),
            pl.BlockSpec((1, 1, w, HEAD_DIM), lambda bi, hk, i: (bi, hk, right(i), 0)),
            pl.BlockSpec((1, 1, V_ROWS, w), lambda bi, hk, i: (bi, hk, 0, left(i))),
            pl.BlockSpec((1, 1, V_ROWS, span), lambda bi, hk, i: (bi, hk, 0, i)),
            pl.BlockSpec((1, 1, V_ROWS, w), lambda bi, hk, i: (bi, hk, 0, right(i))),
            pl.BlockSpec((1, GROUP, 1, span), lambda bi, hk, i: (bi, hk, 0, i)),
        ],
        out_specs=pl.BlockSpec((1, span, GROUP * HEAD_DIM), lambda bi, hk, i: (bi, i, hk)),
        compiler_params=pltpu.CompilerParams(
            dimension_semantics=("arbitrary", "arbitrary", "arbitrary"),
            vmem_limit_bytes=V7X_VMEM_LIMIT),
        name="attn_window",
    )(sink, _bound_stats(qn, kn), bias, qt, k, k, k, vt, vt, vt, qn)


def _channel_kernel(x_ref, mod_ref, ya_ref, yb_ref, sg_ref, wb_ref, wo_ref,
                    g2_ref, gf_ref, w1_ref, w2_ref, o_ref, *, n_sub, n_chunks):
    d = x_ref.shape[2]
    ts = x_ref.shape[1] // n_sub
    cw = w1_ref.shape[1] // n_chunks
    dot = functools.partial(jnp.dot, preferred_element_type=F32)
    gate1 = mod_ref[0, 2:3, :]
    shift2 = mod_ref[0, 3:4, :]
    scale2 = mod_ref[0, 4:5, :]
    gate2 = mod_ref[0, 5:6, :]
    for sub in range(n_sub):
        tok = slice(sub * ts, (sub + 1) * ts)
        ua = dot(ya_ref[0, tok, :], wb_ref[0])
        ub = dot(yb_ref[0, tok, :], wb_ref[1])
        sg = sg_ref[0, tok, :]
        merged = sg[:, :d].astype(F32) * ua + sg[:, d:].astype(F32) * ub
        x1 = x_ref[0, tok, :] + gate1 * dot(merged.astype(BF16), wo_ref[...])

        h2 = (_rms(x1) * g2_ref[...] * (1.0 + scale2) + shift2).astype(BF16)
        acc = jnp.zeros((ts, d), F32)
        for c in range(n_chunks):
            hid = dot(h2, w1_ref[:, c * cw:(c + 1) * cw])
            hid = jnp.square(jnp.maximum(hid, 0.0)).astype(BF16)
            acc = acc + dot(hid, w2_ref[c * cw:(c + 1) * cw, :])
        x2 = x1 + gate2 * acc
        o_ref[0, tok, :] = _rms(x2) * gf_ref[...]


def _channel_call(x, mod3, ya, yb, sg, wb, wo, g2, gf, w1, w2, *, tm=512, n_sub=2, n_chunks=4):
    b, s, d = x.shape
    bw = ya.shape[2]
    ff = w1.shape[1]
    tok = lambda bi, i: (bi, i, 0)
    const2 = lambda bi, i: (0, 0)
    once = pl.Buffered(1)
    return pl.pallas_call(
        functools.partial(_channel_kernel, n_sub=n_sub, n_chunks=n_chunks),
        out_shape=jax.ShapeDtypeStruct((b, s, d), F32),
        grid=(b, s // tm),
        in_specs=[
            pl.BlockSpec((1, tm, d), tok),
            pl.BlockSpec((1, 6, d), lambda bi, i: (bi, 0, 0)),
            pl.BlockSpec((1, tm, bw), tok),
            pl.BlockSpec((1, tm, bw), tok),
            pl.BlockSpec((1, tm, 2 * d), tok),
            pl.BlockSpec((2, bw, d), lambda bi, i: (0, 0, 0), pipeline_mode=once),
            pl.BlockSpec((d, d), const2, pipeline_mode=once),
            pl.BlockSpec((1, d), const2),
            pl.BlockSpec((1, d), const2),
            pl.BlockSpec((d, ff), const2, pipeline_mode=once),
            pl.BlockSpec((ff, d), const2, pipeline_mode=once),
        ],
        out_specs=pl.BlockSpec((1, tm, d), tok),
        compiler_params=pltpu.CompilerParams(
            dimension_semantics=("arbitrary", "arbitrary"),
            vmem_limit_bytes=V7X_VMEM_LIMIT),
        name="merge_mlp_final_norm",
    )(x, mod3, ya, yb, sg, wb, wo, g2, gf, w1, w2)


def _rope_tables(s):
    def cos_sin(pos, dim, theta):
        inv = theta ** (-np.arange(0, dim, 2, dtype=np.float64) / dim)
        ang = pos.astype(np.float64)[:, None] * inv[None, :]
        return np.cos(ang).T, np.sin(ang).T

    t = np.arange(s)
    cr, sr = cos_sin(t // GRID_W, HEAD_DIM // 2, AXIAL_THETA)
    cc, sc = cos_sin(t % GRID_W, HEAD_DIM // 2, AXIAL_THETA)
    c1, s1 = cos_sin(t, HEAD_DIM, ROPE_THETA)
    tables = (np.concatenate([cr, cr, cc, cc]), np.concatenate([-sr, sr, -sc, sc]),
              np.concatenate([c1, c1]), np.concatenate([-s1, s1]))
    return tuple(jnp.asarray(tab.astype(np.float32)) for tab in tables)


def kernel(x, c, w_ada, b_ada, norm1_g, w_in, q_norm_a, k_norm_a, sink_b, w_branch, w_out,
           norm2_g, w_mlp_in, w_mlp_out, final_g):
    b, s, d = x.shape
    assert w_ada.shape[0] == 1, "kernel supports DEPTH == 1"
    l = 0
    n_qkv = 2 * (Q_HEADS + 2 * KV_HEADS) * HEAD_DIM
    cos_a, sin_a, cos_b, sin_b = _rope_tables(s)
    c_pad = jnp.pad(c, ((0, 8 - b), (0, 0)))

    mod = _mod_call(c_pad, w_ada[l], b_ada[l][None, :])
    mod3 = mod[:b].reshape(b, 6, d)
    wt = w_in[l][:, :n_qkv].T.astype(BF16)
    wg = w_in[l][:, n_qkv:].astype(BF16)
    qa, ka, va, qna, kna, qb, kb, vb, qnb, knb, sg = _proj_call(
        x, mod3, norm1_g[l][None, :], wt, wg,
        q_norm_a[l][:, None], k_norm_a[l][:, None], cos_a, sin_a, cos_b, sin_b)
    ya = _attn_a_call(qa, ka, va, qna, kna)
    yb = _attn_b_call(sink_b[l], qb, kb, vb, qnb, knb)
    return _channel_call(x, mod3, ya, yb, sg, w_branch[l].astype(BF16), w_out[l].astype(BF16),
                         norm2_g[l][None, :], final_g[None, :],
                         w_mlp_in[l].astype(BF16), w_mlp_out[l].astype(BF16))
```

```python
import functools
import math

import jax
import jax.numpy as jnp
import numpy as np
from jax import lax
from jax.experimental import pallas as pl
from jax.experimental.pallas import tpu as pltpu

F32 = jnp.float32
BF16 = jnp.bfloat16

HEAD_DIM = 64
Q_HEADS = 8
KV_HEADS = 2
GROUP = Q_HEADS // KV_HEADS
GRID_W = 64
WINDOW = 128
ROPE_THETA = 10000.0
AXIAL_THETA = 10000.0
NORM_EPS = 1e-6
NEG_INF = -1e30
LOG2E = math.log2(math.e)

V7X_BF16_SUBLANES = 16
V7X_VMEM_LIMIT = 56 * 1024 * 1024

V_ROWS = HEAD_DIM + V7X_BF16_SUBLANES

_BF16_SQUARE_MARGIN = (1.0 + 2.0 ** -8) ** 2


def _rms(x, eps=NORM_EPS):
    return x * lax.rsqrt(jnp.mean(x * x, axis=-1, keepdims=True) + eps)


def _mod_kernel(c_ref, w_ref, b_ref, o_ref):
    c = c_ref[...]
    a = c * (1.0 / (1.0 + jnp.exp(-c)))
    w = w_ref[...]
    a_hi = a.astype(BF16)
    a_lo = (a - a_hi.astype(F32)).astype(BF16)
    w_hi = w.astype(BF16)
    w_lo = (w - w_hi.astype(F32)).astype(BF16)
    dot = functools.partial(jnp.dot, preferred_element_type=F32)
    o_ref[...] = dot(a_hi, w_hi) + (dot(a_hi, w_lo) + dot(a_lo, w_hi)) + b_ref[...]


def _mod_call(c_pad, w_ada, b_ada, *, tn=1024):
    rows, d = c_pad.shape
    n = w_ada.shape[1]
    return pl.pallas_call(
        _mod_kernel,
        out_shape=jax.ShapeDtypeStruct((rows, n), F32),
        grid=(n // tn,),
        in_specs=[
            pl.BlockSpec((rows, d), lambda j: (0, 0)),
            pl.BlockSpec((d, tn), lambda j: (0, j)),
            pl.BlockSpec((1, tn), lambda j: (0, j)),
        ],
        out_specs=pl.BlockSpec((rows, tn), lambda j: (0, j)),
        compiler_params=pltpu.CompilerParams(
            dimension_semantics=("arbitrary",), vmem_limit_bytes=V7X_VMEM_LIMIT),
        name="adaln_mod",
    )(c_pad, w_ada, b_ada)


def _rope_t(x, cos, sin, half):
    parts = []
    for base in range(0, HEAD_DIM, 2 * half):
        parts.append(x[base + half:base + 2 * half])
        parts.append(x[base:base + half])
    x_rot = jnp.concatenate(parts, axis=0)
    return x * cos + x_rot * sin


def _proj_kernel(x_ref, mod_ref, g1_ref, wt_ref, wg_ref, gq_ref, gk_ref,
                 cos_a_ref, sin_a_ref, cos_b_ref, sin_b_ref,
                 qa_ref, ka_ref, va_ref, qna_ref, kna_ref,
                 qb_ref, kb_ref, vb_ref, qnb_ref, knb_ref, sg_ref, *, n_sub):
    shift1 = mod_ref[0, 0:1, :]
    scale1 = mod_ref[0, 1:2, :]
    g1 = g1_ref[...]
    gq, gk = gq_ref[...], gk_ref[...]
    q_scale = HEAD_DIM ** -0.5 * LOG2E
    ts = x_ref.shape[1] // n_sub
    ones = jnp.ones((V_ROWS - HEAD_DIM, ts), BF16)

    def normed(t, g):
        ms = jnp.sum(t * t, axis=0, keepdims=True) * (1.0 / HEAD_DIM)
        return t * lax.rsqrt(ms + NORM_EPS) * g

    def norm2_bound(t):
        return jnp.sum(t * t, axis=0, keepdims=True) * _BF16_SQUARE_MARGIN

    for sub in range(n_sub):
        tok = slice(sub * ts, (sub + 1) * ts)
        x = x_ref[0, tok, :]
        h = (_rms(x) * g1 * (1.0 + scale1) + shift1).astype(BF16)

        pt = lax.dot_general(wt_ref[...], h, (((1,), (1,)), ((), ())),
                             preferred_element_type=F32)
        cos_a, sin_a = cos_a_ref[:, tok], sin_a_ref[:, tok]
        cos_b, sin_b = cos_b_ref[:, tok], sin_b_ref[:, tok]

        def head(row):
            return pt[row:row + HEAD_DIM]

        row = 0
        for hq in range(Q_HEADS):
            t = _rope_t(normed(head(row), gq), cos_a, sin_a, HEAD_DIM // 4) * q_scale
            qa_ref[0, hq, :, tok] = t.astype(BF16)
            qna_ref[0, hq, :, tok] = norm2_bound(t)
            row += HEAD_DIM
        for hk in range(KV_HEADS):
            t = _rope_t(normed(head(row), gk), cos_a, sin_a, HEAD_DIM // 4)
            ka_ref[0, hk, tok, :] = t.T.astype(BF16)
            kna_ref[0, hk, :, tok] = norm2_bound(t)
            row += HEAD_DIM
        for hk in range(KV_HEADS):
            va_ref[0, hk, :, tok] = jnp.concatenate([head(row).astype(BF16), ones], axis=0)
            row += HEAD_DIM
        for hq in range(Q_HEADS):
            t = _rope_t(head(row), cos_b, sin_b, HEAD_DIM // 2) * q_scale
            qb_ref[0, hq, :, tok] = t.astype(BF16)
            qnb_ref[0, hq, :, tok] = norm2_bound(t)
            row += HEAD_DIM
        for hk in range(KV_HEADS):
            t = _rope_t(head(row), cos_b, sin_b, HEAD_DIM // 2)
            kb_ref[0, hk, tok, :] = t.T.astype(BF16)
            knb_ref[0, hk, :, tok] = norm2_bound(t)
            row += HEAD_DIM
        for hk in range(KV_HEADS):
            vb_ref[0, hk, :, tok] = jnp.concatenate([head(row).astype(BF16), ones], axis=0)
            row += HEAD_DIM

        g = jnp.dot(h, wg_ref[...], preferred_element_type=F32)
        sg_ref[0, tok, :] = (1.0 / (1.0 + jnp.exp(-g))).astype(BF16)


def _proj_call(x, mod3, g1, wt, wg, gq, gk, cos_a, sin_a, cos_b, sin_b, *, tm=1024, n_sub=4):
    b, s, d = x.shape
    n_qkv = wt.shape[0]
    n_g = wg.shape[1]
    const2 = lambda bi, i: (0, 0)
    tab = pl.BlockSpec((HEAD_DIM, tm), lambda bi, i: (0, i))
    q_shape = jax.ShapeDtypeStruct((b, Q_HEADS, HEAD_DIM, s), BF16)
    k_shape = jax.ShapeDtypeStruct((b, KV_HEADS, s, HEAD_DIM), BF16)
    v_shape = jax.ShapeDtypeStruct((b, KV_HEADS, V_ROWS, s), BF16)
    q_spec = pl.BlockSpec((1, Q_HEADS, HEAD_DIM, tm), lambda bi, i: (bi, 0, 0, i))
    k_spec = pl.BlockSpec((1, KV_HEADS, tm, HEAD_DIM), lambda bi, i: (bi, 0, i, 0))
    v_spec = pl.BlockSpec((1, KV_HEADS, V_ROWS, tm), lambda bi, i: (bi, 0, 0, i))
    qn_shape = jax.ShapeDtypeStruct((b, Q_HEADS, 1, s), F32)
    qn_spec = pl.BlockSpec((1, Q_HEADS, 1, tm), lambda bi, i: (bi, 0, 0, i))
    kn_shape = jax.ShapeDtypeStruct((b, KV_HEADS, 1, s), F32)
    kn_spec = pl.BlockSpec((1, KV_HEADS, 1, tm), lambda bi, i: (bi, 0, 0, i))
    mixer_shapes = (q_shape, k_shape, v_shape, qn_shape, kn_shape)
    mixer_specs = (q_spec, k_spec, v_spec, qn_spec, kn_spec)
    return pl.pallas_call(
        functools.partial(_proj_kernel, n_sub=n_sub),
        out_shape=mixer_shapes + mixer_shapes + (jax.ShapeDtypeStruct((b, s, n_g), BF16),),
        grid=(b, s // tm),
        in_specs=[
            pl.BlockSpec((1, tm, d), lambda bi, i: (bi, i, 0)),
            pl.BlockSpec((1, 6, d), lambda bi, i: (bi, 0, 0)),
            pl.BlockSpec((1, d), const2),
            pl.BlockSpec((n_qkv, d), const2),
            pl.BlockSpec((d, n_g), const2),
            pl.BlockSpec((HEAD_DIM, 1), const2),
            pl.BlockSpec((HEAD_DIM, 1), const2),
            tab, tab, tab, tab,
        ],
        out_specs=mixer_specs + mixer_specs + (
            pl.BlockSpec((1, tm, n_g), lambda bi, i: (bi, i, 0)),),
        compiler_params=pltpu.CompilerParams(
            dimension_semantics=("arbitrary", "arbitrary"),
            vmem_limit_bytes=V7X_VMEM_LIMIT),
        name="in_proj",
    )(x, mod3, g1, wt, wg, gq, gk, cos_a, sin_a, cos_b, sin_b)


_MAX_FIXED_STABILISER = 50.0


def _attn_a_finish(acc, o_ref, tq):
    o = acc[:HEAD_DIM] * (1.0 / acc[HEAD_DIM:HEAD_DIM + 1])
    o_ref[0] = jnp.concatenate(
        [o[:, hq * tq:(hq + 1) * tq].T for hq in range(GROUP)], axis=1).astype(BF16)


def _bound_stats(qn, kn):
    b = qn.shape[0]
    kmax2 = jnp.max(kn, axis=(2, 3))
    qmax2 = jnp.max(qn.reshape(b, KV_HEADS, -1), axis=2)
    return jnp.stack([kmax2, qmax2 * kmax2], axis=-1).reshape(-1)


def _read_bound_stats(stats_ref):
    base = (pl.program_id(0) * KV_HEADS + pl.program_id(1)) * 2
    return stats_ref[base], stats_ref[base + 1] <= _MAX_FIXED_STABILISER ** 2


def _attn_a_kernel(stats_ref, q_ref, k_ref, v_ref, qn_ref, o_ref, *, tk, unroll):
    tq = q_ref.shape[3]
    n = GROUP * tq
    n_chunks = k_ref.shape[2] // tk
    q = jnp.concatenate([q_ref[0, hq] for hq in range(GROUP)], axis=1)
    kmax2, fixed_ok = _read_bound_stats(stats_ref)

    def chunk(c):
        start = pl.multiple_of(c * tk, tk)
        return k_ref[0, 0, pl.ds(start, tk), :], v_ref[0, 0, :, pl.ds(start, tk)]

    @pl.when(fixed_ok)
    def _():
        qn2 = jnp.concatenate([qn_ref[0, hq] for hq in range(GROUP)], axis=1)
        bound = jnp.sqrt(qn2 * kmax2)

        def scores(c):
            return jnp.dot(chunk(c)[0], q, preferred_element_type=F32)

        def consume(st, c, acc):
            p = jnp.exp2(st - bound).astype(BF16)
            return acc + jnp.dot(chunk(c)[1], p, preferred_element_type=F32)

        def body(c, carry):
            st, acc = carry
            st_next = scores(c + 1)
            return st_next, consume(st, c, acc)

        init = (scores(0), jnp.zeros((V_ROWS, n), F32))
        st, acc = lax.fori_loop(0, n_chunks - 1, body, init, unroll=unroll)
        _attn_a_finish(consume(st, n_chunks - 1, acc), o_ref, tq)

    @pl.when(jnp.logical_not(fixed_ok))
    def _():
        def body(c, carry):
            m, acc = carry
            kc, vc = chunk(c)
            st = jnp.dot(kc, q, preferred_element_type=F32)
            m_new = jnp.maximum(m, jnp.max(st, axis=0, keepdims=True))
            p = jnp.exp2(st - m_new).astype(BF16)
            acc = acc * jnp.exp2(m - m_new) + jnp.dot(vc, p, preferred_element_type=F32)
            return m_new, acc

        init = (jnp.full((1, n), NEG_INF, F32), jnp.zeros((V_ROWS, n), F32))
        _, acc = lax.fori_loop(0, n_chunks, body, init)
        _attn_a_finish(acc, o_ref, tq)


def _attn_a_call(qt, k, vt, qn, kn, *, tq=256, tk=256, unroll=True):
    b, _, _, s = qt.shape
    return pl.pallas_call(
        functools.partial(_attn_a_kernel, tk=tk, unroll=unroll),
        out_shape=jax.ShapeDtypeStruct((b, s, Q_HEADS * HEAD_DIM), BF16),
        grid=(b, KV_HEADS, s // tq),
        in_specs=[
            pl.BlockSpec(memory_space=pltpu.SMEM),
            pl.BlockSpec((1, GROUP, HEAD_DIM, tq), lambda bi, hk, i: (bi, hk, 0, i)),
            pl.BlockSpec((1, 1, s, HEAD_DIM), lambda bi, hk, i: (bi, hk, 0, 0)),
            pl.BlockSpec((1, 1, V_ROWS, s), lambda bi, hk, i: (bi, hk, 0, 0)),
            pl.BlockSpec((1, GROUP, 1, tq), lambda bi, hk, i: (bi, hk, 0, i)),
        ],
        out_specs=pl.BlockSpec((1, tq, GROUP * HEAD_DIM), lambda bi, hk, i: (bi, i, hk)),
        compiler_params=pltpu.CompilerParams(
            dimension_semantics=("arbitrary", "arbitrary", "arbitrary"),
            vmem_limit_bytes=V7X_VMEM_LIMIT),
        name="attn_global",
    )(_bound_stats(qn, kn), qt, k, vt, qn)


def _attn_b_kernel(sink_ref, stats_ref, bias_ref, q_ref, kl_ref, kc_ref, kr_ref,
                   vl_ref, vc_ref, vr_ref, qn_ref, o_ref, *, nsub):
    hk = pl.program_id(1)
    i = pl.program_id(2)
    last_i = pl.num_programs(2) - 1
    w = kl_ref.shape[2]
    k = jnp.concatenate([kl_ref[0, 0], kc_ref[0, 0], kr_ref[0, 0]], axis=0)
    v = jnp.concatenate([vl_ref[0, 0], vc_ref[0, 0], vr_ref[0, 0]], axis=1)
    sink = jnp.concatenate(
        [jnp.full((1, w), sink_ref[hk * GROUP + hq] * LOG2E, F32) for hq in range(GROUP)],
        axis=1)
    band = bias_ref[1]
    kmax2, fixed_ok = _read_bound_stats(stats_ref)

    def block_lanes(ref, j):
        return jnp.concatenate([ref[0, hq, :, j * w:(j + 1) * w] for hq in range(GROUP)], axis=1)

    def masked_scores(j):
        bias = band
        if j == 0:
            bias = jnp.where(i == 0, bias_ref[0], band)
        if j == nsub - 1:
            bias = jnp.where(i == last_i, bias_ref[2], bias)
        st = jnp.dot(k[j * w:(j + 3) * w], block_lanes(q_ref, j),
                     preferred_element_type=F32)
        return st + jnp.concatenate([bias] * GROUP, axis=1)

    def finish(j, st, m):
        p = jnp.exp2(st - m).astype(BF16)
        acc = jnp.dot(v[:, j * w:(j + 3) * w], p, preferred_element_type=F32)
        denom = acc[HEAD_DIM:HEAD_DIM + 1] + jnp.exp2(sink - m)
        o = acc[:HEAD_DIM] * (1.0 / denom)
        o_ref[0, j * w:(j + 1) * w, :] = jnp.concatenate(
            [o[:, hq * w:(hq + 1) * w].T for hq in range(GROUP)], axis=1).astype(BF16)

    @pl.when(fixed_ok)
    def _():
        ahead = 2
        sts = [masked_scores(j) for j in range(min(ahead, nsub))]
        for j in range(nsub):
            if j + ahead < nsub:
                sts.append(masked_scores(j + ahead))
            bound = jnp.sqrt(block_lanes(qn_ref, j) * kmax2)
            finish(j, sts[j], jnp.maximum(bound, sink))

    @pl.when(jnp.logical_not(fixed_ok))
    def _():
        for j in range(nsub):
            st = masked_scores(j)
            finish(j, st, jnp.maximum(jnp.max(st, axis=0, keepdims=True), sink))


def _band_bias():
    w = WINDOW
    r = np.arange(3 * w)[:, None] - w
    c = np.arange(w)[None, :]
    band = np.abs(r - c) <= w
    first = band & (r >= 0)
    last = band & (r < w)
    table = np.stack([first, band, last]).astype(np.float32)
    return (1.0 - table) * NEG_INF


def _attn_b_call(sink, qt, k, vt, qn, kn, *, nsub=16):
    b, _, _, s = qt.shape
    w = WINDOW
    nb = s // w
    span = nsub * w
    bias = jnp.asarray(_band_bias(), F32)
    left = lambda i: jnp.maximum(i * nsub - 1, 0)
    right = lambda i: jnp.minimum((i + 1) * nsub, nb - 1)

    return pl.pallas_call(
        functools.partial(_attn_b_kernel, nsub=nsub),
        out_shape=jax.ShapeDtypeStruct((b, s, Q_HEADS * HEAD_DIM), BF16),
        grid=(b, KV_HEADS, s // span),
        in_specs=[
            pl.BlockSpec(memory_space=pltpu.SMEM),
            pl.BlockSpec(memory_space=pltpu.SMEM),
            pl.BlockSpec((3, 3 * w, w), lambda bi, hk, i: (0, 0, 0)),
            pl.BlockSpec((1, GROUP, HEAD_DIM, span), lambda bi, hk, i: (bi, hk, 0, i)),
            pl.BlockSpec((1, 1, w, HEAD_DIM), lambda bi, hk, i: (bi, hk, left(i), 0)),
            pl.BlockSpec((1, 1, span, HEAD_DIM), lambda bi, hk, i: (bi, hk, i, 0)),
            pl.BlockSpec((1, 1, w, HEAD_DIM), lambda bi, hk, i: (bi, hk, right(i), 0)),
            pl.BlockSpec((1, 1, V_ROWS, w), lambda bi, hk, i: (bi, hk, 0, left(i))),
            pl.BlockSpec((1, 1, V_ROWS, span), lambda bi, hk, i: (bi, hk, 0, i)),
            pl.BlockSpec((1, 1, V_ROWS, w), lambda bi, hk, i: (bi, hk, 0, right(i))),
            pl.BlockSpec((1, GROUP, 1, span), lambda bi, hk, i: (bi, hk, 0, i)),
        ],
        out_specs=pl.BlockSpec((1, span, GROUP * HEAD_DIM), lambda bi, hk, i: (bi, i, hk)),
        compiler_params=pltpu.CompilerParams(
            dimension_semantics=("arbitrary", "arbitrary", "arbitrary"),
            vmem_limit_bytes=V7X_VMEM_LIMIT),
        name="attn_window",
    )(sink, _bound_stats(qn, kn), bias, qt, k, k, k, vt, vt, vt, qn)


def _channel_kernel(x_ref, mod_ref, ya_ref, yb_ref, sg_ref, wb_ref, wo_ref,
                    g2_ref, gf_ref, w1_ref, w2_ref, o_ref, *, n_sub, n_chunks):
    d = x_ref.shape[2]
    ts = x_ref.shape[1] // n_sub
    cw = w1_ref.shape[1] // n_chunks
    dot = functools.partial(jnp.dot, preferred_element_type=F32)
    gate1 = mod_ref[0, 2:3, :]
    shift2 = mod_ref[0, 3:4, :]
    scale2 = mod_ref[0, 4:5, :]
    gate2 = mod_ref[0, 5:6, :]
    for sub in range(n_sub):
        tok = slice(sub * ts, (sub + 1) * ts)
        ua = dot(ya_ref[0, tok, :], wb_ref[0])
        ub = dot(yb_ref[0, tok, :], wb_ref[1])
        sg = sg_ref[0, tok, :]
        merged = sg[:, :d].astype(F32) * ua + sg[:, d:].astype(F32) * ub
        x1 = x_ref[0, tok, :] + gate1 * dot(merged.astype(BF16), wo_ref[...])

        h2 = (_rms(x1) * g2_ref[...] * (1.0 + scale2) + shift2).astype(BF16)
        acc = jnp.zeros((ts, d), F32)
        for c in range(n_chunks):
            hid = dot(h2, w1_ref[:, c * cw:(c + 1) * cw])
            hid = jnp.square(jnp.maximum(hid, 0.0)).astype(BF16)
            acc = acc + dot(hid, w2_ref[c * cw:(c + 1) * cw, :])
        x2 = x1 + gate2 * acc
        o_ref[0, tok, :] = _rms(x2) * gf_ref[...]


def _channel_call(x, mod3, ya, yb, sg, wb, wo, g2, gf, w1, w2, *, tm=1024, n_sub=4, n_chunks=4):
    b, s, d = x.shape
    bw = ya.shape[2]
    ff = w1.shape[1]
    tok = lambda bi, i: (bi, i, 0)
    const2 = lambda bi, i: (0, 0)
    once = pl.Buffered(1)
    return pl.pallas_call(
        functools.partial(_channel_kernel, n_sub=n_sub, n_chunks=n_chunks),
        out_shape=jax.ShapeDtypeStruct((b, s, d), F32),
        grid=(b, s // tm),
        in_specs=[
            pl.BlockSpec((1, tm, d), tok),
            pl.BlockSpec((1, 6, d), lambda bi, i: (bi, 0, 0)),
            pl.BlockSpec((1, tm, bw), tok),
            pl.BlockSpec((1, tm, bw), tok),
            pl.BlockSpec((1, tm, 2 * d), tok),
            pl.BlockSpec((2, bw, d), lambda bi, i: (0, 0, 0), pipeline_mode=once),
            pl.BlockSpec((d, d), const2, pipeline_mode=once),
            pl.BlockSpec((1, d), const2),
            pl.BlockSpec((1, d), const2),
            pl.BlockSpec((d, ff), const2, pipeline_mode=once),
            pl.BlockSpec((ff, d), const2, pipeline_mode=once),
        ],
        out_specs=pl.BlockSpec((1, tm, d), tok),
        compiler_params=pltpu.CompilerParams(
            dimension_semantics=("arbitrary", "arbitrary"),
            vmem_limit_bytes=V7X_VMEM_LIMIT),
        name="merge_mlp_final_norm",
    )(x, mod3, ya, yb, sg, wb, wo, g2, gf, w1, w2)


def _rope_tables(s):
    def cos_sin(pos, dim, theta):
        inv = theta ** (-np.arange(0, dim, 2, dtype=np.float64) / dim)
        ang = pos.astype(np.float64)[:, None] * inv[None, :]
        return np.cos(ang).T, np.sin(ang).T

    t = np.arange(s)
    cr, sr = cos_sin(t // GRID_W, HEAD_DIM // 2, AXIAL_THETA)
    cc, sc = cos_sin(t % GRID_W, HEAD_DIM // 2, AXIAL_THETA)
    c1, s1 = cos_sin(t, HEAD_DIM, ROPE_THETA)
    tables = (np.concatenate([cr, cr, cc, cc]), np.concatenate([-sr, sr, -sc, sc]),
              np.concatenate([c1, c1]), np.concatenate([-s1, s1]))
    return tuple(jnp.asarray(tab.astype(np.float32)) for tab in tables)


def kernel(x, c, w_ada, b_ada, norm1_g, w_in, q_norm_a, k_norm_a, sink_b, w_branch, w_out,
           norm2_g, w_mlp_in, w_mlp_out, final_g):
    b, s, d = x.shape
    assert w_ada.shape[0] == 1, "kernel supports DEPTH == 1"
    l = 0
    n_qkv = 2 * (Q_HEADS + 2 * KV_HEADS) * HEAD_DIM
    cos_a, sin_a, cos_b, sin_b = _rope_tables(s)
    c_pad = jnp.pad(c, ((0, 8 - b), (0, 0)))

    mod = _mod_call(c_pad, w_ada[l], b_ada[l][None, :])
    mod3 = mod[:b].reshape(b, 6, d)
    wt = w_in[l][:, :n_qkv].T.astype(BF16)
    wg = w_in[l][:, n_qkv:].astype(BF16)
    qa, ka, va, qna, kna, qb, kb, vb, qnb, knb, sg = _proj_call(
        x, mod3, norm1_g[l][None, :], wt, wg,
        q_norm_a[l][:, None], k_norm_a[l][:, None], cos_a, sin_a, cos_b, sin_b)
    ya = _attn_a_call(qa, ka, va, qna, kna)
    yb = _attn_b_call(sink_b[l], qb, kb, vb, qnb, knb)
    return _channel_call(x, mod3, ya, yb, sg, w_branch[l].astype(BF16), w_out[l].astype(BF16),
                         norm2_g[l][None, :], final_g[None, :],
                         w_mlp_in[l].astype(BF16), w_mlp_out[l].astype(BF16))
```

```python
import functools
import math

import jax
import jax.numpy as jnp
import numpy as np
from jax import lax
from jax.experimental import pallas as pl
from jax.experimental.pallas import tpu as pltpu

F32 = jnp.float32
BF16 = jnp.bfloat16

HEAD_DIM = 64
Q_HEADS = 8
KV_HEADS = 2
GROUP = Q_HEADS // KV_HEADS
GRID_W = 64
WINDOW = 128
ROPE_THETA = 10000.0
AXIAL_THETA = 10000.0
NORM_EPS = 1e-6
NEG_INF = -1e30
LOG2E = math.log2(math.e)

V7X_BF16_SUBLANES = 16
V7X_VMEM_LIMIT = 56 * 1024 * 1024

V_ROWS = HEAD_DIM + V7X_BF16_SUBLANES

_BF16_SQUARE_MARGIN = (1.0 + 2.0 ** -8) ** 2


def _rms(x, eps=NORM_EPS):
    return x * lax.rsqrt(jnp.mean(x * x, axis=-1, keepdims=True) + eps)


def _mod_kernel(c_ref, w_ref, b_ref, o_ref):
    c = c_ref[...]
    a = c * (1.0 / (1.0 + jnp.exp(-c)))
    w = w_ref[...]
    a_hi = a.astype(BF16)
    a_lo = (a - a_hi.astype(F32)).astype(BF16)
    w_hi = w.astype(BF16)
    w_lo = (w - w_hi.astype(F32)).astype(BF16)
    dot = functools.partial(jnp.dot, preferred_element_type=F32)
    o_ref[...] = dot(a_hi, w_hi) + (dot(a_hi, w_lo) + dot(a_lo, w_hi)) + b_ref[...]


def _mod_call(c_pad, w_ada, b_ada, *, tn=1024):
    rows, d = c_pad.shape
    n = w_ada.shape[1]
    return pl.pallas_call(
        _mod_kernel,
        out_shape=jax.ShapeDtypeStruct((rows, n), F32),
        grid=(n // tn,),
        in_specs=[
            pl.BlockSpec((rows, d), lambda j: (0, 0)),
            pl.BlockSpec((d, tn), lambda j: (0, j)),
            pl.BlockSpec((1, tn), lambda j: (0, j)),
        ],
        out_specs=pl.BlockSpec((rows, tn), lambda j: (0, j)),
        compiler_params=pltpu.CompilerParams(
            dimension_semantics=("arbitrary",), vmem_limit_bytes=V7X_VMEM_LIMIT),
        name="adaln_mod",
    )(c_pad, w_ada, b_ada)


def _rope_t(x, cos, sin, half):
    parts = []
    for base in range(0, HEAD_DIM, 2 * half):
        parts.append(x[base + half:base + 2 * half])
        parts.append(x[base:base + half])
    x_rot = jnp.concatenate(parts, axis=0)
    return x * cos + x_rot * sin


def _proj_kernel(x_ref, mod_ref, g1_ref, wt_ref, wg_ref, gq_ref, gk_ref,
                 cos_a_ref, sin_a_ref, cos_b_ref, sin_b_ref,
                 qa_ref, ka_ref, va_ref, qna_ref, kna_ref,
                 qb_ref, kb_ref, vb_ref, qnb_ref, knb_ref, sg_ref, *, n_sub):
    shift1 = mod_ref[0, 0:1, :]
    scale1 = mod_ref[0, 1:2, :]
    g1 = g1_ref[...]
    gq, gk = gq_ref[...], gk_ref[...]
    q_scale = HEAD_DIM ** -0.5 * LOG2E
    ts = x_ref.shape[1] // n_sub
    ones = jnp.ones((V_ROWS - HEAD_DIM, ts), BF16)

    def normed(t, g):
        ms = jnp.sum(t * t, axis=0, keepdims=True) * (1.0 / HEAD_DIM)
        return t * lax.rsqrt(ms + NORM_EPS) * g

    def norm2_bound(t):
        return jnp.sum(t * t, axis=0, keepdims=True) * _BF16_SQUARE_MARGIN

    for sub in range(n_sub):
        tok = slice(sub * ts, (sub + 1) * ts)
        x = x_ref[0, tok, :]
        h = (_rms(x) * g1 * (1.0 + scale1) + shift1).astype(BF16)

        pt = lax.dot_general(wt_ref[...], h, (((1,), (1,)), ((), ())),
                             preferred_element_type=F32)
        cos_a, sin_a = cos_a_ref[:, tok], sin_a_ref[:, tok]
        cos_b, sin_b = cos_b_ref[:, tok], sin_b_ref[:, tok]

        def head(row):
            return pt[row:row + HEAD_DIM]

        row = 0
        for hq in range(Q_HEADS):
            t = _rope_t(normed(head(row), gq), cos_a, sin_a, HEAD_DIM // 4) * q_scale
            qa_ref[0, hq, :, tok] = t.astype(BF16)
            qna_ref[0, hq, :, tok] = norm2_bound(t)
            row += HEAD_DIM
        for hk in range(KV_HEADS):
            t = _rope_t(normed(head(row), gk), cos_a, sin_a, HEAD_DIM // 4)
            ka_ref[0, hk, tok, :] = t.T.astype(BF16)
            kna_ref[0, hk, :, tok] = norm2_bound(t)
            row += HEAD_DIM
        for hk in range(KV_HEADS):
            va_ref[0, hk, :, tok] = head(row).astype(BF16)
            row += HEAD_DIM
        for hq in range(Q_HEADS):
            t = _rope_t(head(row), cos_b, sin_b, HEAD_DIM // 2) * q_scale
            qb_ref[0, hq, :, tok] = t.astype(BF16)
            qnb_ref[0, hq, :, tok] = norm2_bound(t)
            row += HEAD_DIM
        for hk in range(KV_HEADS):
            t = _rope_t(head(row), cos_b, sin_b, HEAD_DIM // 2)
            kb_ref[0, hk, tok, :] = t.T.astype(BF16)
            knb_ref[0, hk, :, tok] = norm2_bound(t)
            row += HEAD_DIM
        for hk in range(KV_HEADS):
            vb_ref[0, hk, :, tok] = jnp.concatenate([head(row).astype(BF16), ones], axis=0)
            row += HEAD_DIM

        g = jnp.dot(h, wg_ref[...], preferred_element_type=F32)
        sg_ref[0, tok, :] = (1.0 / (1.0 + jnp.exp(-g))).astype(BF16)


def _proj_call(x, mod3, g1, wt, wg, gq, gk, cos_a, sin_a, cos_b, sin_b, *, tm=1024, n_sub=4):
    b, s, d = x.shape
    n_qkv = wt.shape[0]
    n_g = wg.shape[1]
    const2 = lambda bi, i: (0, 0)
    tab = pl.BlockSpec((HEAD_DIM, tm), lambda bi, i: (0, i))
    q_shape = jax.ShapeDtypeStruct((b, Q_HEADS, HEAD_DIM, s), BF16)
    k_shape = jax.ShapeDtypeStruct((b, KV_HEADS, s, HEAD_DIM), BF16)
    v_shape = jax.ShapeDtypeStruct((b, KV_HEADS, V_ROWS, s), BF16)
    q_spec = pl.BlockSpec((1, Q_HEADS, HEAD_DIM, tm), lambda bi, i: (bi, 0, 0, i))
    k_spec = pl.BlockSpec((1, KV_HEADS, tm, HEAD_DIM), lambda bi, i: (bi, 0, i, 0))
    v_spec = pl.BlockSpec((1, KV_HEADS, V_ROWS, tm), lambda bi, i: (bi, 0, 0, i))
    qn_shape = jax.ShapeDtypeStruct((b, Q_HEADS, 1, s), F32)
    qn_spec = pl.BlockSpec((1, Q_HEADS, 1, tm), lambda bi, i: (bi, 0, 0, i))
    kn_shape = jax.ShapeDtypeStruct((b, KV_HEADS, 1, s), F32)
    kn_spec = pl.BlockSpec((1, KV_HEADS, 1, tm), lambda bi, i: (bi, 0, 0, i))
    va_shape = jax.ShapeDtypeStruct((b, KV_HEADS, HEAD_DIM, s), BF16)
    va_spec = pl.BlockSpec((1, KV_HEADS, HEAD_DIM, tm), lambda bi, i: (bi, 0, 0, i))
    a_shapes = (q_shape, k_shape, va_shape, qn_shape, kn_shape)
    a_specs = (q_spec, k_spec, va_spec, qn_spec, kn_spec)
    b_shapes = (q_shape, k_shape, v_shape, qn_shape, kn_shape)
    b_specs = (q_spec, k_spec, v_spec, qn_spec, kn_spec)
    return pl.pallas_call(
        functools.partial(_proj_kernel, n_sub=n_sub),
        out_shape=a_shapes + b_shapes + (jax.ShapeDtypeStruct((b, s, n_g), BF16),),
        grid=(b, s // tm),
        in_specs=[
            pl.BlockSpec((1, tm, d), lambda bi, i: (bi, i, 0)),
            pl.BlockSpec((1, 6, d), lambda bi, i: (bi, 0, 0)),
            pl.BlockSpec((1, d), const2),
            pl.BlockSpec((n_qkv, d), const2),
            pl.BlockSpec((d, n_g), const2),
            pl.BlockSpec((HEAD_DIM, 1), const2),
            pl.BlockSpec((HEAD_DIM, 1), const2),
            tab, tab, tab, tab,
        ],
        out_specs=a_specs + b_specs + (
            pl.BlockSpec((1, tm, n_g), lambda bi, i: (bi, i, 0)),),
        compiler_params=pltpu.CompilerParams(
            dimension_semantics=("arbitrary", "arbitrary"),
            vmem_limit_bytes=V7X_VMEM_LIMIT),
        name="in_proj",
    )(x, mod3, g1, wt, wg, gq, gk, cos_a, sin_a, cos_b, sin_b)


_MAX_FIXED_STABILISER = 50.0


def _attn_a_finish(acc, l, o_ref, tq):
    o = acc * (1.0 / l)
    o_ref[0] = jnp.concatenate(
        [o[:, hq * tq:(hq + 1) * tq].T for hq in range(GROUP)], axis=1).astype(BF16)


def _bound_stats(qn, kn):
    b = qn.shape[0]
    kmax2 = jnp.max(kn, axis=(2, 3))
    qmax2 = jnp.max(qn.reshape(b, KV_HEADS, -1), axis=2)
    return jnp.stack([kmax2, qmax2 * kmax2], axis=-1).reshape(-1)


def _read_bound_stats(stats_ref):
    base = (pl.program_id(0) * KV_HEADS + pl.program_id(1)) * 2
    return stats_ref[base], stats_ref[base + 1] <= _MAX_FIXED_STABILISER ** 2


def _attn_a_kernel(stats_ref, q_ref, k_ref, v_ref, qn_ref, o_ref, *, tk, unroll):
    tq = q_ref.shape[3]
    n = GROUP * tq
    n_chunks = k_ref.shape[2] // tk
    q = jnp.concatenate([q_ref[0, hq] for hq in range(GROUP)], axis=1)
    kmax2, fixed_ok = _read_bound_stats(stats_ref)

    def chunk(c):
        start = pl.multiple_of(c * tk, tk)
        return k_ref[0, 0, pl.ds(start, tk), :], v_ref[0, 0, :, pl.ds(start, tk)]

    @pl.when(fixed_ok)
    def _():
        qn2 = jnp.concatenate([qn_ref[0, hq] for hq in range(GROUP)], axis=1)
        bound = jnp.sqrt(qn2 * kmax2)

        def scores(c):
            return jnp.dot(chunk(c)[0], q, preferred_element_type=F32)

        def consume(st, c, acc_l):
            acc, l = acc_l
            e = jnp.exp2(st - bound)
            l = l + jnp.sum(e.reshape(tk // 8, 8, n), axis=0)
            return acc + jnp.dot(chunk(c)[1], e.astype(BF16), preferred_element_type=F32), l

        def body(c, carry):
            st, acc_l = carry
            st_next = scores(c + 1)
            return st_next, consume(st, c, acc_l)

        init = (scores(0), (jnp.zeros((HEAD_DIM, n), F32), jnp.zeros((8, n), F32)))
        st, acc_l = lax.fori_loop(0, n_chunks - 1, body, init, unroll=unroll)
        acc, l = consume(st, n_chunks - 1, acc_l)
        _attn_a_finish(acc, jnp.sum(l, axis=0, keepdims=True), o_ref, tq)

    @pl.when(jnp.logical_not(fixed_ok))
    def _():
        def body(c, carry):
            m, l, acc = carry
            kc, vc = chunk(c)
            st = jnp.dot(kc, q, preferred_element_type=F32)
            m_new = jnp.maximum(m, jnp.max(st, axis=0, keepdims=True))
            alpha = jnp.exp2(m - m_new)
            e = jnp.exp2(st - m_new)
            l = l * alpha + jnp.sum(e, axis=0, keepdims=True)
            acc = acc * alpha + jnp.dot(vc, e.astype(BF16), preferred_element_type=F32)
            return m_new, l, acc

        init = (jnp.full((1, n), NEG_INF, F32), jnp.zeros((1, n), F32),
                jnp.zeros((HEAD_DIM, n), F32))
        _, l, acc = lax.fori_loop(0, n_chunks, body, init)
        _attn_a_finish(acc, l, o_ref, tq)


def _attn_a_call(qt, k, vt, qn, kn, *, tq=256, tk=256, unroll=True):
    b, _, _, s = qt.shape
    return pl.pallas_call(
        functools.partial(_attn_a_kernel, tk=tk, unroll=unroll),
        out_shape=jax.ShapeDtypeStruct((b, s, Q_HEADS * HEAD_DIM), BF16),
        grid=(b, KV_HEADS, s // tq),
        in_specs=[
            pl.BlockSpec(memory_space=pltpu.SMEM),
            pl.BlockSpec((1, GROUP, HEAD_DIM, tq), lambda bi, hk, i: (bi, hk, 0, i)),
            pl.BlockSpec((1, 1, s, HEAD_DIM), lambda bi, hk, i: (bi, hk, 0, 0)),
            pl.BlockSpec((1, 1, HEAD_DIM, s), lambda bi, hk, i: (bi, hk, 0, 0)),
            pl.BlockSpec((1, GROUP, 1, tq), lambda bi, hk, i: (bi, hk, 0, i)),
        ],
        out_specs=pl.BlockSpec((1, tq, GROUP * HEAD_DIM), lambda bi, hk, i: (bi, i, hk)),
        compiler_params=pltpu.CompilerParams(
            dimension_semantics=("arbitrary", "arbitrary", "arbitrary"),
            vmem_limit_bytes=V7X_VMEM_LIMIT),
        name="attn_global",
    )(_bound_stats(qn, kn), qt, k, vt, qn)


def _attn_b_kernel(sink_ref, stats_ref, bias_ref, q_ref, kl_ref, kc_ref, kr_ref,
                   vl_ref, vc_ref, vr_ref, qn_ref, o_ref, *, nsub):
    hk = pl.program_id(1)
    i = pl.program_id(2)
    last_i = pl.num_programs(2) - 1
    w = kl_ref.shape[2]
    k = jnp.concatenate([kl_ref[0, 0], kc_ref[0, 0], kr_ref[0, 0]], axis=0)
    v = jnp.concatenate([vl_ref[0, 0], vc_ref[0, 0], vr_ref[0, 0]], axis=1)
    sink = jnp.concatenate(
        [jnp.full((1, w), sink_ref[hk * GROUP + hq] * LOG2E, F32) for hq in range(GROUP)],
        axis=1)
    band = bias_ref[1]
    kmax2, fixed_ok = _read_bound_stats(stats_ref)

    def block_lanes(ref, j):
        return jnp.concatenate([ref[0, hq, :, j * w:(j + 1) * w] for hq in range(GROUP)], axis=1)

    def masked_scores(j):
        bias = band
        if j == 0:
            bias = jnp.where(i == 0, bias_ref[0], band)
        if j == nsub - 1:
            bias = jnp.where(i == last_i, bias_ref[2], bias)
        st = jnp.dot(k[j * w:(j + 3) * w], block_lanes(q_ref, j),
                     preferred_element_type=F32)
        return st + jnp.concatenate([bias] * GROUP, axis=1)

    def finish(j, st, m):
        p = jnp.exp2(st - m).astype(BF16)
        acc = jnp.dot(v[:, j * w:(j + 3) * w], p, preferred_element_type=F32)
        denom = acc[HEAD_DIM:HEAD_DIM + 1] + jnp.exp2(sink - m)
        o = acc[:HEAD_DIM] * (1.0 / denom)
        o_ref[0, j * w:(j + 1) * w, :] = jnp.concatenate(
            [o[:, hq * w:(hq + 1) * w].T for hq in range(GROUP)], axis=1).astype(BF16)

    @pl.when(fixed_ok)
    def _():
        ahead = 2
        sts = [masked_scores(j) for j in range(min(ahead, nsub))]
        for j in range(nsub):
            if j + ahead < nsub:
                sts.append(masked_scores(j + ahead))
            bound = jnp.sqrt(block_lanes(qn_ref, j) * kmax2)
            finish(j, sts[j], jnp.maximum(bound, sink))

    @pl.when(jnp.logical_not(fixed_ok))
    def _():
        for j in range(nsub):
            st = masked_scores(j)
            finish(j, st, jnp.maximum(jnp.max(st, axis=0, keepdims=True), sink))


def _band_bias():
    w = WINDOW
    r = np.arange(3 * w)[:, None] - w
    c = np.arange(w)[None, :]
    band = np.abs(r - c) <= w
    first = band & (r >= 0)
    last = band & (r < w)
    table = np.stack([first, band, last]).astype(np.float32)
    return (1.0 - table) * NEG_INF


def _attn_b_call(sink, qt, k, vt, qn, kn, *, nsub=16):
    b, _, _, s = qt.shape
    w = WINDOW
    nb = s // w
    span = nsub * w
    bias = jnp.asarray(_band_bias(), F32)
    left = lambda i: jnp.maximum(i * nsub - 1, 0)
    right = lambda i: jnp.minimum((i + 1) * nsub, nb - 1)

    return pl.pallas_call(
        functools.partial(_attn_b_kernel, nsub=nsub),
        out_shape=jax.ShapeDtypeStruct((b, s, Q_HEADS * HEAD_DIM), BF16),
        grid=(b, KV_HEADS, s // span),
        in_specs=[
            pl.BlockSpec(memory_space=pltpu.SMEM),
            pl.BlockSpec(memory_space=pltpu.SMEM),
            pl.BlockSpec((3, 3 * w, w), lambda bi, hk, i: (0, 0, 0)),
            pl.BlockSpec((1, GROUP, HEAD_DIM, span), lambda bi, hk, i: (bi, hk, 0, i)),
            pl.BlockSpec((1, 1, w, HEAD_DIM), lambda bi, hk, i: (bi, hk, left(i), 0)),
            pl.BlockSpec((1, 1, span, HEAD_DIM), lambda bi, hk, i: (bi, hk, i, 0)),
            pl.BlockSpec((1, 1, w, HEAD_DIM), lambda bi, hk, i: (bi, hk, right(i), 0)),
            pl.BlockSpec((1, 1, V_ROWS, w), lambda bi, hk, i: (bi, hk, 0, left(i))),
            pl.BlockSpec((1, 1, V_ROWS, span), lambda bi, hk, i: (bi, hk, 0, i)),
            pl.BlockSpec((1, 1, V_ROWS, w), lambda bi, hk, i: (bi, hk, 0, right(i))),
            pl.BlockSpec((1, GROUP, 1, span), lambda bi, hk, i: (bi, hk, 0, i)),
        ],
        out_specs=pl.BlockSpec((1, span, GROUP * HEAD_DIM), lambda bi, hk, i: (bi, i, hk)),
        compiler_params=pltpu.CompilerParams(
            dimension_semantics=("arbitrary", "arbitrary", "arbitrary"),
            vmem_limit_bytes=V7X_VMEM_LIMIT),
        name="attn_window",
    )(sink, _bound_stats(qn, kn), bias, qt, k, k, k, vt, vt, vt, qn)


def _channel_kernel(x_ref, mod_ref, ya_ref, yb_ref, sg_ref, wb_ref, wo_ref,
                    g2_ref, gf_ref, w1_ref, w2_ref, o_ref, *, n_sub, n_chunks):
    d = x_ref.shape[2]
    ts = x_ref.shape[1] // n_sub
    cw = w1_ref.shape[1] // n_chunks
    dot = functools.partial(jnp.dot, preferred_element_type=F32)
    gate1 = mod_ref[0, 2:3, :]
    shift2 = mod_ref[0, 3:4, :]
    scale2 = mod_ref[0, 4:5, :]
    gate2 = mod_ref[0, 5:6, :]
    for sub in range(n_sub):
        tok = slice(sub * ts, (sub + 1) * ts)
        ua = dot(ya_ref[0, tok, :], wb_ref[0])
        ub = dot(yb_ref[0, tok, :], wb_ref[1])
        sg = sg_ref[0, tok, :]
        merged = sg[:, :d].astype(F32) * ua + sg[:, d:].astype(F32) * ub
        x1 = x_ref[0, tok, :] + gate1 * dot(merged.astype(BF16), wo_ref[...])

        h2 = (_rms(x1) * g2_ref[...] * (1.0 + scale2) + shift2).astype(BF16)
        acc = jnp.zeros((ts, d), F32)
        for c in range(n_chunks):
            hid = dot(h2, w1_ref[:, c * cw:(c + 1) * cw])
            hid = jnp.square(jnp.maximum(hid, 0.0)).astype(BF16)
            acc = acc + dot(hid, w2_ref[c * cw:(c + 1) * cw, :])
        x2 = x1 + gate2 * acc
        o_ref[0, tok, :] = _rms(x2) * gf_ref[...]


def _channel_call(x, mod3, ya, yb, sg, wb, wo, g2, gf, w1, w2, *, tm=1024, n_sub=2, n_chunks=4):
    b, s, d = x.shape
    bw = ya.shape[2]
    ff = w1.shape[1]
    tok = lambda bi, i: (bi, i, 0)
    const2 = lambda bi, i: (0, 0)
    once = pl.Buffered(1)
    return pl.pallas_call(
        functools.partial(_channel_kernel, n_sub=n_sub, n_chunks=n_chunks),
        out_shape=jax.ShapeDtypeStruct((b, s, d), F32),
        grid=(b, s // tm),
        in_specs=[
            pl.BlockSpec((1, tm, d), tok),
            pl.BlockSpec((1, 6, d), lambda bi, i: (bi, 0, 0)),
            pl.BlockSpec((1, tm, bw), tok),
            pl.BlockSpec((1, tm, bw), tok),
            pl.BlockSpec((1, tm, 2 * d), tok),
            pl.BlockSpec((2, bw, d), lambda bi, i: (0, 0, 0), pipeline_mode=once),
            pl.BlockSpec((d, d), const2, pipeline_mode=once),
            pl.BlockSpec((1, d), const2),
            pl.BlockSpec((1, d), const2),
            pl.BlockSpec((d, ff), const2, pipeline_mode=once),
            pl.BlockSpec((ff, d), const2, pipeline_mode=once),
        ],
        out_specs=pl.BlockSpec((1, tm, d), tok),
        compiler_params=pltpu.CompilerParams(
            dimension_semantics=("arbitrary", "arbitrary"),
            vmem_limit_bytes=V7X_VMEM_LIMIT),
        name="merge_mlp_final_norm",
    )(x, mod3, ya, yb, sg, wb, wo, g2, gf, w1, w2)


def _rope_tables(s):
    def cos_sin(pos, dim, theta):
        inv = theta ** (-np.arange(0, dim, 2, dtype=np.float64) / dim)
        ang = pos.astype(np.float64)[:, None] * inv[None, :]
        return np.cos(ang).T, np.sin(ang).T

    t = np.arange(s)
    cr, sr = cos_sin(t // GRID_W, HEAD_DIM // 2, AXIAL_THETA)
    cc, sc = cos_sin(t % GRID_W, HEAD_DIM // 2, AXIAL_THETA)
    c1, s1 = cos_sin(t, HEAD_DIM, ROPE_THETA)
    tables = (np.concatenate([cr, cr, cc, cc]), np.concatenate([-sr, sr, -sc, sc]),
              np.concatenate([c1, c1]), np.concatenate([-s1, s1]))
    return tuple(jnp.asarray(tab.astype(np.float32)) for tab in tables)


def kernel(x, c, w_ada, b_ada, norm1_g, w_in, q_norm_a, k_norm_a, sink_b, w_branch, w_out,
           norm2_g, w_mlp_in, w_mlp_out, final_g):
    b, s, d = x.shape
    assert w_ada.shape[0] == 1, "kernel supports DEPTH == 1"
    l = 0
    n_qkv = 2 * (Q_HEADS + 2 * KV_HEADS) * HEAD_DIM
    cos_a, sin_a, cos_b, sin_b = _rope_tables(s)
    c_pad = jnp.pad(c, ((0, 8 - b), (0, 0)))

    mod = _mod_call(c_pad, w_ada[l], b_ada[l][None, :])
    mod3 = mod[:b].reshape(b, 6, d)
    wt = w_in[l][:, :n_qkv].T.astype(BF16)
    wg = w_in[l][:, n_qkv:].astype(BF16)
    qa, ka, va, qna, kna, qb, kb, vb, qnb, knb, sg = _proj_call(
        x, mod3, norm1_g[l][None, :], wt, wg,
        q_norm_a[l][:, None], k_norm_a[l][:, None], cos_a, sin_a, cos_b, sin_b)
    ya = _attn_a_call(qa, ka, va, qna, kna)
    yb = _attn_b_call(sink_b[l], qb, kb, vb, qnb, knb)
    return _channel_call(x, mod3, ya, yb, sg, w_branch[l].astype(BF16), w_out[l].astype(BF16),
                         norm2_g[l][None, :], final_g[None, :],
                         w_mlp_in[l].astype(BF16), w_mlp_out[l].astype(BF16))
```

```python
import functools
import math

import jax
import jax.numpy as jnp
import numpy as np
from jax import lax
from jax.experimental import pallas as pl
from jax.experimental.pallas import tpu as pltpu

F32 = jnp.float32
BF16 = jnp.bfloat16

HEAD_DIM = 64
Q_HEADS = 8
KV_HEADS = 2
GROUP = Q_HEADS // KV_HEADS
GRID_W = 64
WINDOW = 128
ROPE_THETA = 10000.0
AXIAL_THETA = 10000.0
NORM_EPS = 1e-6
NEG_INF = -1e30
LOG2E = math.log2(math.e)

V7X_BF16_SUBLANES = 16
V7X_VMEM_LIMIT = 56 * 1024 * 1024

V_ROWS = HEAD_DIM + V7X_BF16_SUBLANES

_BF16_SQUARE_MARGIN = (1.0 + 2.0 ** -8) ** 2


def _rms(x, eps=NORM_EPS):
    return x * lax.rsqrt(jnp.mean(x * x, axis=-1, keepdims=True) + eps)


def _mod_kernel(c_ref, w_ref, b_ref, o_ref):
    c = c_ref[...]
    a = c * (1.0 / (1.0 + jnp.exp(-c)))
    w = w_ref[...]
    a_hi = a.astype(BF16)
    a_lo = (a - a_hi.astype(F32)).astype(BF16)
    w_hi = w.astype(BF16)
    w_lo = (w - w_hi.astype(F32)).astype(BF16)
    dot = functools.partial(jnp.dot, preferred_element_type=F32)
    o_ref[...] = dot(a_hi, w_hi) + (dot(a_hi, w_lo) + dot(a_lo, w_hi)) + b_ref[...]


def _mod_call(c_pad, w_ada, b_ada, *, tn=1024):
    rows, d = c_pad.shape
    n = w_ada.shape[1]
    return pl.pallas_call(
        _mod_kernel,
        out_shape=jax.ShapeDtypeStruct((rows, n), F32),
        grid=(n // tn,),
        in_specs=[
            pl.BlockSpec((rows, d), lambda j: (0, 0)),
            pl.BlockSpec((d, tn), lambda j: (0, j)),
            pl.BlockSpec((1, tn), lambda j: (0, j)),
        ],
        out_specs=pl.BlockSpec((rows, tn), lambda j: (0, j)),
        compiler_params=pltpu.CompilerParams(
            dimension_semantics=("arbitrary",), vmem_limit_bytes=V7X_VMEM_LIMIT),
        name="adaln_mod",
    )(c_pad, w_ada, b_ada)


def _rope_t(x, cos, sin, half):
    parts = []
    for base in range(0, HEAD_DIM, 2 * half):
        parts.append(x[base + half:base + 2 * half])
        parts.append(x[base:base + half])
    x_rot = jnp.concatenate(parts, axis=0)
    return x * cos + x_rot * sin


def _proj_kernel(x_ref, mod_ref, g1_ref, w_ref, gq_ref, gk_ref,
                 cos_a_ref, sin_a_ref, cos_b_ref, sin_b_ref,
                 qa_ref, ka_ref, va_ref, qna_ref, kna_ref,
                 qb_ref, kb_ref, vb_ref, qnb_ref, knb_ref, sg_ref, wt_ref, *, n_sub, n_qkv):
    @pl.when((pl.program_id(0) == 0) & (pl.program_id(1) == 0))
    def _():
        wt_ref[...] = w_ref[:, :n_qkv].T

    shift1 = mod_ref[0, 0:1, :]
    scale1 = mod_ref[0, 1:2, :]
    g1 = g1_ref[...]
    gq, gk = gq_ref[...], gk_ref[...]
    q_scale = HEAD_DIM ** -0.5 * LOG2E
    ts = x_ref.shape[1] // n_sub
    ones = jnp.ones((V_ROWS - HEAD_DIM, ts), BF16)

    def normed(t, g):
        ms = jnp.sum(t * t, axis=0, keepdims=True) * (1.0 / HEAD_DIM)
        return t * lax.rsqrt(ms + NORM_EPS) * g

    def norm2_bound(t):
        return jnp.sum(t * t, axis=0, keepdims=True) * _BF16_SQUARE_MARGIN

    for sub in range(n_sub):
        tok = slice(sub * ts, (sub + 1) * ts)
        x = x_ref[0, tok, :]
        h = (_rms(x) * g1 * (1.0 + scale1) + shift1).astype(BF16)

        pt = lax.dot_general(wt_ref[...], h, (((1,), (1,)), ((), ())),
                             preferred_element_type=F32)
        cos_a, sin_a = cos_a_ref[:, tok], sin_a_ref[:, tok]
        cos_b, sin_b = cos_b_ref[:, tok], sin_b_ref[:, tok]

        def head(row):
            return pt[row:row + HEAD_DIM]

        row = 0
        for hq in range(Q_HEADS):
            t = _rope_t(normed(head(row), gq), cos_a, sin_a, HEAD_DIM // 4) * q_scale
            qa_ref[0, hq, :, tok] = t.astype(BF16)
            qna_ref[0, hq, :, tok] = norm2_bound(t)
            row += HEAD_DIM
        for hk in range(KV_HEADS):
            t = _rope_t(normed(head(row), gk), cos_a, sin_a, HEAD_DIM // 4)
            ka_ref[0, hk, tok, :] = t.T.astype(BF16)
            kna_ref[0, hk, :, tok] = norm2_bound(t)
            row += HEAD_DIM
        for hk in range(KV_HEADS):
            va_ref[0, hk, :, tok] = jnp.concatenate([head(row).astype(BF16), ones], axis=0)
            row += HEAD_DIM
        for hq in range(Q_HEADS):
            t = _rope_t(head(row), cos_b, sin_b, HEAD_DIM // 2) * q_scale
            qb_ref[0, hq, :, tok] = t.astype(BF16)
            qnb_ref[0, hq, :, tok] = norm2_bound(t)
            row += HEAD_DIM
        for hk in range(KV_HEADS):
            t = _rope_t(head(row), cos_b, sin_b, HEAD_DIM // 2)
            kb_ref[0, hk, tok, :] = t.T.astype(BF16)
            knb_ref[0, hk, :, tok] = norm2_bound(t)
            row += HEAD_DIM
        for hk in range(KV_HEADS):
            vb_ref[0, hk, :, tok] = jnp.concatenate([head(row).astype(BF16), ones], axis=0)
            row += HEAD_DIM

        g = jnp.dot(h, w_ref[:, n_qkv:], preferred_element_type=F32)
        sg_ref[0, tok, :] = (1.0 / (1.0 + jnp.exp(-g))).astype(BF16)


def _proj_call(x, mod3, g1, w, gq, gk, cos_a, sin_a, cos_b, sin_b, *, tm=1024, n_sub=4):
    b, s, d = x.shape
    n_qkv = 2 * (Q_HEADS + 2 * KV_HEADS) * HEAD_DIM
    n_g = w.shape[1] - n_qkv
    const2 = lambda bi, i: (0, 0)
    tab = pl.BlockSpec((HEAD_DIM, tm), lambda bi, i: (0, i))
    q_shape = jax.ShapeDtypeStruct((b, Q_HEADS, HEAD_DIM, s), BF16)
    k_shape = jax.ShapeDtypeStruct((b, KV_HEADS, s, HEAD_DIM), BF16)
    v_shape = jax.ShapeDtypeStruct((b, KV_HEADS, V_ROWS, s), BF16)
    q_spec = pl.BlockSpec((1, Q_HEADS, HEAD_DIM, tm), lambda bi, i: (bi, 0, 0, i))
    k_spec = pl.BlockSpec((1, KV_HEADS, tm, HEAD_DIM), lambda bi, i: (bi, 0, i, 0))
    v_spec = pl.BlockSpec((1, KV_HEADS, V_ROWS, tm), lambda bi, i: (bi, 0, 0, i))
    qn_shape = jax.ShapeDtypeStruct((b, Q_HEADS, 1, s), F32)
    qn_spec = pl.BlockSpec((1, Q_HEADS, 1, tm), lambda bi, i: (bi, 0, 0, i))
    kn_shape = jax.ShapeDtypeStruct((b, KV_HEADS, 1, s), F32)
    kn_spec = pl.BlockSpec((1, KV_HEADS, 1, tm), lambda bi, i: (bi, 0, 0, i))
    mixer_shapes = (q_shape, k_shape, v_shape, qn_shape, kn_shape)
    mixer_specs = (q_spec, k_spec, v_spec, qn_spec, kn_spec)
    return pl.pallas_call(
        functools.partial(_proj_kernel, n_sub=n_sub, n_qkv=n_qkv),
        out_shape=mixer_shapes + mixer_shapes + (jax.ShapeDtypeStruct((b, s, n_g), BF16),),
        grid=(b, s // tm),
        in_specs=[
            pl.BlockSpec((1, tm, d), lambda bi, i: (bi, i, 0)),
            pl.BlockSpec((1, 6, d), lambda bi, i: (bi, 0, 0)),
            pl.BlockSpec((1, d), const2),
            pl.BlockSpec((d, n_qkv + n_g), const2, pipeline_mode=pl.Buffered(1)),
            pl.BlockSpec((HEAD_DIM, 1), const2),
            pl.BlockSpec((HEAD_DIM, 1), const2),
            tab, tab, tab, tab,
        ],
        out_specs=mixer_specs + mixer_specs + (
            pl.BlockSpec((1, tm, n_g), lambda bi, i: (bi, i, 0)),),
        scratch_shapes=[pltpu.VMEM((n_qkv, d), BF16)],
        compiler_params=pltpu.CompilerParams(
            dimension_semantics=("arbitrary", "arbitrary"),
            vmem_limit_bytes=V7X_VMEM_LIMIT),
        name="in_proj",
    )(x, mod3, g1, w, gq, gk, cos_a, sin_a, cos_b, sin_b)


_MAX_FIXED_STABILISER = 50.0


def _attn_a_finish(acc, o_ref, tq):
    o = acc[:HEAD_DIM] * (1.0 / acc[HEAD_DIM:HEAD_DIM + 1])
    o_ref[0] = jnp.concatenate(
        [o[:, hq * tq:(hq + 1) * tq].T for hq in range(GROUP)], axis=1).astype(BF16)


def _bound_stats(qn, kn):
    b = qn.shape[0]
    kmax2 = jnp.max(kn, axis=(2, 3))
    qmax2 = jnp.max(qn.reshape(b, KV_HEADS, -1), axis=2)
    return jnp.stack([kmax2, qmax2 * kmax2], axis=-1).reshape(-1)


def _read_bound_stats(stats_ref):
    base = (pl.program_id(0) * KV_HEADS + pl.program_id(1)) * 2
    return stats_ref[base], stats_ref[base + 1] <= _MAX_FIXED_STABILISER ** 2


def _attn_a_kernel(stats_ref, q_ref, k_ref, v_ref, qn_ref, o_ref, *, tk, ahead, unroll):
    tq = q_ref.shape[3]
    n = GROUP * tq
    n_chunks = k_ref.shape[2] // tk
    q = jnp.concatenate([q_ref[0, hq] for hq in range(GROUP)], axis=1)
    kmax2, fixed_ok = _read_bound_stats(stats_ref)

    def chunk(c):
        start = pl.multiple_of(c * tk, tk)
        return k_ref[0, 0, pl.ds(start, tk), :], v_ref[0, 0, :, pl.ds(start, tk)]

    @pl.when(fixed_ok)
    def _():
        qn2 = jnp.concatenate([qn_ref[0, hq] for hq in range(GROUP)], axis=1)
        bound = jnp.sqrt(qn2 * kmax2)

        def scores(c):
            return jnp.dot(chunk(c)[0], q, preferred_element_type=F32)

        def consume(st, c, acc):
            p = jnp.exp2(st - bound).astype(BF16)
            return acc + jnp.dot(chunk(c)[1], p, preferred_element_type=F32)

        def body(c, carry):
            sts, acc = carry
            return sts[1:] + (scores(c + ahead),), consume(sts[0], c, acc)

        init = (tuple(scores(c) for c in range(ahead)), jnp.zeros((V_ROWS, n), F32))
        sts, acc = lax.fori_loop(0, n_chunks - ahead, body, init, unroll=unroll)
        for j in range(ahead):
            acc = consume(sts[j], n_chunks - ahead + j, acc)
        _attn_a_finish(acc, o_ref, tq)

    @pl.when(jnp.logical_not(fixed_ok))
    def _():
        def body(c, carry):
            m, acc = carry
            kc, vc = chunk(c)
            st = jnp.dot(kc, q, preferred_element_type=F32)
            m_new = jnp.maximum(m, jnp.max(st, axis=0, keepdims=True))
            p = jnp.exp2(st - m_new).astype(BF16)
            acc = acc * jnp.exp2(m - m_new) + jnp.dot(vc, p, preferred_element_type=F32)
            return m_new, acc

        init = (jnp.full((1, n), NEG_INF, F32), jnp.zeros((V_ROWS, n), F32))
        _, acc = lax.fori_loop(0, n_chunks, body, init)
        _attn_a_finish(acc, o_ref, tq)


def _attn_a_call(qt, k, vt, qn, kn, *, tq=256, tk=256, ahead=1, unroll=True):
    b, _, _, s = qt.shape
    return pl.pallas_call(
        functools.partial(_attn_a_kernel, tk=tk, ahead=ahead, unroll=unroll),
        out_shape=jax.ShapeDtypeStruct((b, s, Q_HEADS * HEAD_DIM), BF16),
        grid=(b, KV_HEADS, s // tq),
        in_specs=[
            pl.BlockSpec(memory_space=pltpu.SMEM),
            pl.BlockSpec((1, GROUP, HEAD_DIM, tq), lambda bi, hk, i: (bi, hk, 0, i)),
            pl.BlockSpec((1, 1, s, HEAD_DIM), lambda bi, hk, i: (bi, hk, 0, 0)),
            pl.BlockSpec((1, 1, V_ROWS, s), lambda bi, hk, i: (bi, hk, 0, 0)),
            pl.BlockSpec((1, GROUP, 1, tq), lambda bi, hk, i: (bi, hk, 0, i)),
        ],
        out_specs=pl.BlockSpec((1, tq, GROUP * HEAD_DIM), lambda bi, hk, i: (bi, i, hk)),
        compiler_params=pltpu.CompilerParams(
            dimension_semantics=("arbitrary", "arbitrary", "arbitrary"),
            vmem_limit_bytes=V7X_VMEM_LIMIT),
        name="attn_global",
    )(_bound_stats(qn, kn), qt, k, vt, qn)


def _attn_b_kernel(sink_ref, stats_ref, bias_ref, q_ref, kl_ref, kc_ref, kr_ref,
                   vl_ref, vc_ref, vr_ref, qn_ref, o_ref, *, nsub):
    hk = pl.program_id(1)
    i = pl.program_id(2)
    last_i = pl.num_programs(2) - 1
    w = kl_ref.shape[2]
    k = jnp.concatenate([kl_ref[0, 0], kc_ref[0, 0], kr_ref[0, 0]], axis=0)
    v = jnp.concatenate([vl_ref[0, 0], vc_ref[0, 0], vr_ref[0, 0]], axis=1)
    sink = jnp.concatenate(
        [jnp.full((1, w), sink_ref[hk * GROUP + hq] * LOG2E, F32) for hq in range(GROUP)],
        axis=1)
    band = bias_ref[1]
    kmax2, fixed_ok = _read_bound_stats(stats_ref)

    def block_lanes(ref, j):
        return jnp.concatenate([ref[0, hq, :, j * w:(j + 1) * w] for hq in range(GROUP)], axis=1)

    def masked_scores(j):
        bias = band
        if j == 0:
            bias = jnp.where(i == 0, bias_ref[0], band)
        if j == nsub - 1:
            bias = jnp.where(i == last_i, bias_ref[2], bias)
        st = jnp.dot(k[j * w:(j + 3) * w], block_lanes(q_ref, j),
                     preferred_element_type=F32)
        return st + jnp.concatenate([bias] * GROUP, axis=1)

    def finish(j, st, m):
        p = jnp.exp2(st - m).astype(BF16)
        acc = jnp.dot(v[:, j * w:(j + 3) * w], p, preferred_element_type=F32)
        denom = acc[HEAD_DIM:HEAD_DIM + 1] + jnp.exp2(sink - m)
        o = acc[:HEAD_DIM] * (1.0 / denom)
        o_ref[0, j * w:(j + 1) * w, :] = jnp.concatenate(
            [o[:, hq * w:(hq + 1) * w].T for hq in range(GROUP)], axis=1).astype(BF16)

    @pl.when(fixed_ok)
    def _():
        ahead = 2
        sts = [masked_scores(j) for j in range(min(ahead, nsub))]
        for j in range(nsub):
            if j + ahead < nsub:
                sts.append(masked_scores(j + ahead))
            bound = jnp.sqrt(block_lanes(qn_ref, j) * kmax2)
            finish(j, sts[j], jnp.maximum(bound, sink))

    @pl.when(jnp.logical_not(fixed_ok))
    def _():
        for j in range(nsub):
            st = masked_scores(j)
            finish(j, st, jnp.maximum(jnp.max(st, axis=0, keepdims=True), sink))


def _band_bias():
    w = WINDOW
    r = np.arange(3 * w)[:, None] - w
    c = np.arange(w)[None, :]
    band = np.abs(r - c) <= w
    first = band & (r >= 0)
    last = band & (r < w)
    table = np.stack([first, band, last]).astype(np.float32)
    return (1.0 - table) * NEG_INF


def _attn_b_call(sink, qt, k, vt, qn, kn, *, nsub=16):
    b, _, _, s = qt.shape
    w = WINDOW
    nb = s // w
    span = nsub * w
    bias = jnp.asarray(_band_bias(), F32)
    left = lambda i: jnp.maximum(i * nsub - 1, 0)
    right = lambda i: jnp.minimum((i + 1) * nsub, nb - 1)

    return pl.pallas_call(
        functools.partial(_attn_b_kernel, nsub=nsub),
        out_shape=jax.ShapeDtypeStruct((b, s, Q_HEADS * HEAD_DIM), BF16),
        grid=(b, KV_HEADS, s // span),
        in_specs=[
            pl.BlockSpec(memory_space=pltpu.SMEM),
            pl.BlockSpec(memory_space=pltpu.SMEM),
            pl.BlockSpec((3, 3 * w, w), lambda bi, hk, i: (0, 0, 0)),
            pl.BlockSpec((1, GROUP, HEAD_DIM, span), lambda bi, hk, i: (bi, hk, 0, i)),
            pl.BlockSpec((1, 1, w, HEAD_DIM), lambda bi, hk, i: (bi, hk, left(i), 0)),
            pl.BlockSpec((1, 1, span, HEAD_DIM), lambda bi, hk, i: (bi, hk, i, 0)),
            pl.BlockSpec((1, 1, w, HEAD_DIM), lambda bi, hk, i: (bi, hk, right(i), 0)),
            pl.BlockSpec((1, 1, V_ROWS, w), lambda bi, hk, i: (bi, hk, 0, left(i))),
            pl.BlockSpec((1, 1, V_ROWS, span), lambda bi, hk, i: (bi, hk, 0, i)),
            pl.BlockSpec((1, 1, V_ROWS, w), lambda bi, hk, i: (bi, hk, 0, right(i))),
            pl.BlockSpec((1, GROUP, 1, span), lambda bi, hk, i: (bi, hk, 0, i)),
        ],
        out_specs=pl.BlockSpec((1, span, GROUP * HEAD_DIM), lambda bi, hk, i: (bi, i, hk)),
        compiler_params=pltpu.CompilerParams(
            dimension_semantics=("arbitrary", "arbitrary", "arbitrary"),
            vmem_limit_bytes=V7X_VMEM_LIMIT),
        name="attn_window",
    )(sink, _bound_stats(qn, kn), bias, qt, k, k, k, vt, vt, vt, qn)


def _channel_kernel(x_ref, mod_ref, ya_ref, yb_ref, sg_ref, wb_ref, wo_ref,
                    g2_ref, gf_ref, w1_ref, w2_ref, o_ref, *, n_sub, n_chunks):
    d = x_ref.shape[2]
    ts = x_ref.shape[1] // n_sub
    cw = w1_ref.shape[1] // n_chunks
    dot = functools.partial(jnp.dot, preferred_element_type=F32)
    gate1 = mod_ref[0, 2:3, :]
    shift2 = mod_ref[0, 3:4, :]
    scale2 = mod_ref[0, 4:5, :]
    gate2 = mod_ref[0, 5:6, :]
    for sub in range(n_sub):
        tok = slice(sub * ts, (sub + 1) * ts)
        ua = dot(ya_ref[0, tok, :], wb_ref[0])
        ub = dot(yb_ref[0, tok, :], wb_ref[1])
        sg = sg_ref[0, tok, :]
        merged = sg[:, :d].astype(F32) * ua + sg[:, d:].astype(F32) * ub
        x1 = x_ref[0, tok, :] + gate1 * dot(merged.astype(BF16), wo_ref[...])

        h2 = (_rms(x1) * g2_ref[...] * (1.0 + scale2) + shift2).astype(BF16)
        acc = jnp.zeros((ts, d), F32)
        for c in range(n_chunks):
            hid = dot(h2, w1_ref[:, c * cw:(c + 1) * cw])
            hid = jnp.square(jnp.maximum(hid, 0.0)).astype(BF16)
            acc = acc + dot(hid, w2_ref[c * cw:(c + 1) * cw, :])
        x2 = x1 + gate2 * acc
        o_ref[0, tok, :] = _rms(x2) * gf_ref[...]


def _channel_call(x, mod3, ya, yb, sg, wb, wo, g2, gf, w1, w2, *, tm=1024, n_sub=2, n_chunks=4):
    b, s, d = x.shape
    bw = ya.shape[2]
    ff = w1.shape[1]
    tok = lambda bi, i: (bi, i, 0)
    const2 = lambda bi, i: (0, 0)
    once = pl.Buffered(1)
    return pl.pallas_call(
        functools.partial(_channel_kernel, n_sub=n_sub, n_chunks=n_chunks),
        out_shape=jax.ShapeDtypeStruct((b, s, d), F32),
        grid=(b, s // tm),
        in_specs=[
            pl.BlockSpec((1, tm, d), tok),
            pl.BlockSpec((1, 6, d), lambda bi, i: (bi, 0, 0)),
            pl.BlockSpec((1, tm, bw), tok),
            pl.BlockSpec((1, tm, bw), tok),
            pl.BlockSpec((1, tm, 2 * d), tok),
            pl.BlockSpec((2, bw, d), lambda bi, i: (0, 0, 0), pipeline_mode=once),
            pl.BlockSpec((d, d), const2, pipeline_mode=once),
            pl.BlockSpec((1, d), const2),
            pl.BlockSpec((1, d), const2),
            pl.BlockSpec((d, ff), const2, pipeline_mode=once),
            pl.BlockSpec((ff, d), const2, pipeline_mode=once),
        ],
        out_specs=pl.BlockSpec((1, tm, d), tok),
        compiler_params=pltpu.CompilerParams(
            dimension_semantics=("arbitrary", "arbitrary"),
            vmem_limit_bytes=V7X_VMEM_LIMIT),
        name="merge_mlp_final_norm",
    )(x, mod3, ya, yb, sg, wb, wo, g2, gf, w1, w2)


def _rope_tables(s):
    def cos_sin(pos, dim, theta):
        inv = theta ** (-np.arange(0, dim, 2, dtype=np.float64) / dim)
        ang = pos.astype(np.float64)[:, None] * inv[None, :]
        return np.cos(ang).T, np.sin(ang).T

    t = np.arange(s)
    cr, sr = cos_sin(t // GRID_W, HEAD_DIM // 2, AXIAL_THETA)
    cc, sc = cos_sin(t % GRID_W, HEAD_DIM // 2, AXIAL_THETA)
    c1, s1 = cos_sin(t, HEAD_DIM, ROPE_THETA)
    tables = (np.concatenate([cr, cr, cc, cc]), np.concatenate([-sr, sr, -sc, sc]),
              np.concatenate([c1, c1]), np.concatenate([-s1, s1]))
    return tuple(jnp.asarray(tab.astype(np.float32)) for tab in tables)


def kernel(x, c, w_ada, b_ada, norm1_g, w_in, q_norm_a, k_norm_a, sink_b, w_branch, w_out,
           norm2_g, w_mlp_in, w_mlp_out, final_g):
    b, s, d = x.shape
    assert w_ada.shape[0] == 1, "kernel supports DEPTH == 1"
    l = 0
    cos_a, sin_a, cos_b, sin_b = _rope_tables(s)
    c_pad = jnp.pad(c, ((0, 8 - b), (0, 0)))

    mod = _mod_call(c_pad, w_ada[l], b_ada[l][None, :])
    mod3 = mod[:b].reshape(b, 6, d)
    qa, ka, va, qna, kna, qb, kb, vb, qnb, knb, sg = _proj_call(
        x, mod3, norm1_g[l][None, :], w_in[l].astype(BF16),
        q_norm_a[l][:, None], k_norm_a[l][:, None], cos_a, sin_a, cos_b, sin_b)
    ya = _attn_a_call(qa, ka, va, qna, kna)
    yb = _attn_b_call(sink_b[l], qb, kb, vb, qnb, knb)
    return _channel_call(x, mod3, ya, yb, sg, w_branch[l].astype(BF16), w_out[l].astype(BF16),
                         norm2_g[l][None, :], final_g[None, :],
                         w_mlp_in[l].astype(BF16), w_mlp_out[l].astype(BF16))
```

```python
import functools
import math

import jax
import jax.numpy as jnp
import numpy as np
from jax import lax
from jax.experimental import pallas as pl
from jax.experimental.pallas import tpu as pltpu

F32 = jnp.float32
BF16 = jnp.bfloat16

HEAD_DIM = 64
Q_HEADS = 8
KV_HEADS = 2
GROUP = Q_HEADS // KV_HEADS
GRID_W = 64
WINDOW = 128
ROPE_THETA = 10000.0
AXIAL_THETA = 10000.0
NORM_EPS = 1e-6
NEG_INF = -1e30
LOG2E = math.log2(math.e)

V7X_BF16_SUBLANES = 16
V7X_VMEM_LIMIT = 56 * 1024 * 1024

MOD_COLS = 2048
PROJ_ROWS, PROJ_SUBTILES = 1024, 4
ATTN_A_QUERIES, ATTN_A_KEYS = 256, 256
ATTN_B_BLOCKS = 32
CHANNEL_ROWS, CHANNEL_SUBTILES, MLP_CHUNKS = 1024, 2, 4

V_ROWS = HEAD_DIM + V7X_BF16_SUBLANES

_BF16_SQUARE_MARGIN = (1.0 + 2.0 ** -8) ** 2


def _rms(x, eps=NORM_EPS):
    return x * lax.rsqrt(jnp.mean(x * x, axis=-1, keepdims=True) + eps)


def _mod_kernel(c_ref, w_ref, b_ref, o_ref):
    c = c_ref[...]
    a = c * (1.0 / (1.0 + jnp.exp(-c)))
    w = w_ref[...]
    a_hi = a.astype(BF16)
    a_lo = (a - a_hi.astype(F32)).astype(BF16)
    w_hi = w.astype(BF16)
    w_lo = (w - w_hi.astype(F32)).astype(BF16)
    dot = functools.partial(jnp.dot, preferred_element_type=F32)
    o_ref[...] = dot(a_hi, w_hi) + (dot(a_hi, w_lo) + dot(a_lo, w_hi)) + b_ref[...]


def _mod_call(c, w_ada, b_ada, *, tn=MOD_COLS):
    rows, d = c.shape
    n = w_ada.shape[1]
    return pl.pallas_call(
        _mod_kernel,
        out_shape=jax.ShapeDtypeStruct((rows, n), F32),
        grid=(n // tn,),
        in_specs=[
            pl.BlockSpec((rows, d), lambda j: (0, 0)),
            pl.BlockSpec((d, tn), lambda j: (0, j)),
            pl.BlockSpec((1, tn), lambda j: (0, j)),
        ],
        out_specs=pl.BlockSpec((rows, tn), lambda j: (0, j)),
        compiler_params=pltpu.CompilerParams(
            dimension_semantics=("arbitrary",), vmem_limit_bytes=V7X_VMEM_LIMIT),
        name="adaln_mod",
    )(c, w_ada, b_ada)


def _rope_t(x, cos, sin, half):
    parts = []
    for base in range(0, HEAD_DIM, 2 * half):
        parts.append(x[base + half:base + 2 * half])
        parts.append(x[base:base + half])
    x_rot = jnp.concatenate(parts, axis=0)
    return x * cos + x_rot * sin


def _proj_kernel(x_ref, mod_ref, g1_ref, w_ref, gq_ref, gk_ref,
                 cos_a_ref, sin_a_ref, cos_b_ref, sin_b_ref,
                 qa_ref, ka_ref, va_ref, qna_ref, kna_ref,
                 qb_ref, kb_ref, vb_ref, qnb_ref, knb_ref, sg_ref, wt_ref, *, n_sub, n_qkv):
    @pl.when((pl.program_id(0) == 0) & (pl.program_id(1) == 0))
    def _():
        wt_ref[...] = w_ref[:, :n_qkv].T

    shift1 = mod_ref[0, 0:1, :]
    scale1 = mod_ref[0, 1:2, :]
    g1 = g1_ref[...]
    gq, gk = gq_ref[...], gk_ref[...]
    q_scale = HEAD_DIM ** -0.5 * LOG2E
    ts = x_ref.shape[1] // n_sub
    ones = jnp.ones((V_ROWS - HEAD_DIM, ts), BF16)

    def normed(t, g):
        ms = jnp.sum(t * t, axis=0, keepdims=True) * (1.0 / HEAD_DIM)
        return t * lax.rsqrt(ms + NORM_EPS) * g

    def norm2_bound(t):
        return jnp.sum(t * t, axis=0, keepdims=True) * _BF16_SQUARE_MARGIN

    for sub in range(n_sub):
        tok = slice(sub * ts, (sub + 1) * ts)
        x = x_ref[0, tok, :]
        h = (_rms(x) * g1 * (1.0 + scale1) + shift1).astype(BF16)

        pt = lax.dot_general(wt_ref[...], h, (((1,), (1,)), ((), ())),
                             preferred_element_type=F32)
        cos_a, sin_a = cos_a_ref[:, tok], sin_a_ref[:, tok]
        cos_b, sin_b = cos_b_ref[:, tok], sin_b_ref[:, tok]

        def head(row):
            return pt[row:row + HEAD_DIM]

        row = 0
        for hq in range(Q_HEADS):
            t = _rope_t(normed(head(row), gq), cos_a, sin_a, HEAD_DIM // 4) * q_scale
            qa_ref[0, hq, :, tok] = t.astype(BF16)
            qna_ref[0, hq, :, tok] = norm2_bound(t)
            row += HEAD_DIM
        for hk in range(KV_HEADS):
            t = _rope_t(normed(head(row), gk), cos_a, sin_a, HEAD_DIM // 4)
            ka_ref[0, hk, tok, :] = t.T.astype(BF16)
            kna_ref[0, hk, :, tok] = norm2_bound(t)
            row += HEAD_DIM
        for hk in range(KV_HEADS):
            va_ref[0, hk, :, tok] = jnp.concatenate([head(row).astype(BF16), ones], axis=0)
            row += HEAD_DIM
        for hq in range(Q_HEADS):
            t = _rope_t(head(row), cos_b, sin_b, HEAD_DIM // 2) * q_scale
            qb_ref[0, hq, :, tok] = t.astype(BF16)
            qnb_ref[0, hq, :, tok] = norm2_bound(t)
            row += HEAD_DIM
        for hk in range(KV_HEADS):
            t = _rope_t(head(row), cos_b, sin_b, HEAD_DIM // 2)
            kb_ref[0, hk, tok, :] = t.T.astype(BF16)
            knb_ref[0, hk, :, tok] = norm2_bound(t)
            row += HEAD_DIM
        for hk in range(KV_HEADS):
            vb_ref[0, hk, :, tok] = jnp.concatenate([head(row).astype(BF16), ones], axis=0)
            row += HEAD_DIM

        g = jnp.dot(h, w_ref[:, n_qkv:], preferred_element_type=F32)
        sg_ref[0, tok, :] = (1.0 / (1.0 + jnp.exp(-g))).astype(BF16)


def _proj_call(x, mod3, g1, w, gq, gk, cos_a, sin_a, cos_b, sin_b, *,
               tm=PROJ_ROWS, n_sub=PROJ_SUBTILES):
    b, s, d = x.shape
    assert s % tm == 0 and tm % n_sub == 0
    n_qkv = 2 * (Q_HEADS + 2 * KV_HEADS) * HEAD_DIM
    n_g = w.shape[1] - n_qkv
    const2 = lambda bi, i: (0, 0)
    tab = pl.BlockSpec((HEAD_DIM, tm), lambda bi, i: (0, i))
    q_shape = jax.ShapeDtypeStruct((b, Q_HEADS, HEAD_DIM, s), BF16)
    k_shape = jax.ShapeDtypeStruct((b, KV_HEADS, s, HEAD_DIM), BF16)
    v_shape = jax.ShapeDtypeStruct((b, KV_HEADS, V_ROWS, s), BF16)
    q_spec = pl.BlockSpec((1, Q_HEADS, HEAD_DIM, tm), lambda bi, i: (bi, 0, 0, i))
    k_spec = pl.BlockSpec((1, KV_HEADS, tm, HEAD_DIM), lambda bi, i: (bi, 0, i, 0))
    v_spec = pl.BlockSpec((1, KV_HEADS, V_ROWS, tm), lambda bi, i: (bi, 0, 0, i))
    qn_shape = jax.ShapeDtypeStruct((b, Q_HEADS, 1, s), F32)
    qn_spec = pl.BlockSpec((1, Q_HEADS, 1, tm), lambda bi, i: (bi, 0, 0, i))
    kn_shape = jax.ShapeDtypeStruct((b, KV_HEADS, 1, s), F32)
    kn_spec = pl.BlockSpec((1, KV_HEADS, 1, tm), lambda bi, i: (bi, 0, 0, i))
    mixer_shapes = (q_shape, k_shape, v_shape, qn_shape, kn_shape)
    mixer_specs = (q_spec, k_spec, v_spec, qn_spec, kn_spec)
    return pl.pallas_call(
        functools.partial(_proj_kernel, n_sub=n_sub, n_qkv=n_qkv),
        out_shape=mixer_shapes + mixer_shapes + (jax.ShapeDtypeStruct((b, s, n_g), BF16),),
        grid=(b, s // tm),
        in_specs=[
            pl.BlockSpec((1, tm, d), lambda bi, i: (bi, i, 0)),
            pl.BlockSpec((1, 6, d), lambda bi, i: (bi, 0, 0)),
            pl.BlockSpec((1, d), const2),
            pl.BlockSpec((d, n_qkv + n_g), const2, pipeline_mode=pl.Buffered(1)),
            pl.BlockSpec((HEAD_DIM, 1), const2),
            pl.BlockSpec((HEAD_DIM, 1), const2),
            tab, tab, tab, tab,
        ],
        out_specs=mixer_specs + mixer_specs + (
            pl.BlockSpec((1, tm, n_g), lambda bi, i: (bi, i, 0)),),
        scratch_shapes=[pltpu.VMEM((n_qkv, d), BF16)],
        compiler_params=pltpu.CompilerParams(
            dimension_semantics=("arbitrary", "arbitrary"),
            vmem_limit_bytes=V7X_VMEM_LIMIT),
        name="in_proj",
    )(x, mod3, g1, w, gq, gk, cos_a, sin_a, cos_b, sin_b)


_MAX_FIXED_STABILISER = 50.0


def _attn_a_finish(acc, o_ref, tq):
    o = acc[:HEAD_DIM] * (1.0 / acc[HEAD_DIM:HEAD_DIM + 1])
    o_ref[0] = jnp.concatenate(
        [o[:, hq * tq:(hq + 1) * tq].T for hq in range(GROUP)], axis=1).astype(BF16)


def _bound_stats(qn, kn):
    b = qn.shape[0]
    kmax2 = jnp.max(kn, axis=(2, 3))
    qmax2 = jnp.max(qn.reshape(b, KV_HEADS, -1), axis=2)
    return jnp.stack([kmax2, qmax2 * kmax2], axis=-1).reshape(-1)


def _read_bound_stats(stats_ref):
    base = (pl.program_id(0) * KV_HEADS + pl.program_id(1)) * 2
    return stats_ref[base], stats_ref[base + 1] <= _MAX_FIXED_STABILISER ** 2


def _attn_a_kernel(stats_ref, q_ref, k_ref, v_ref, qn_ref, o_ref, *, tk, ahead, unroll):
    tq = q_ref.shape[3]
    n = GROUP * tq
    n_chunks = k_ref.shape[2] // tk
    q = jnp.concatenate([q_ref[0, hq] for hq in range(GROUP)], axis=1)
    kmax2, fixed_ok = _read_bound_stats(stats_ref)

    def chunk(c):
        start = pl.multiple_of(c * tk, tk)
        return k_ref[0, 0, pl.ds(start, tk), :], v_ref[0, 0, :, pl.ds(start, tk)]

    @pl.when(fixed_ok)
    def _():
        qn2 = jnp.concatenate([qn_ref[0, hq] for hq in range(GROUP)], axis=1)
        bound = jnp.sqrt(qn2 * kmax2)

        def scores(c):
            return jnp.dot(chunk(c)[0], q, preferred_element_type=F32)

        def consume(st, c, acc):
            p = jnp.exp2(st - bound).astype(BF16)
            return acc + jnp.dot(chunk(c)[1], p, preferred_element_type=F32)

        def body(c, carry):
            sts, acc = carry
            return sts[1:] + (scores(c + ahead),), consume(sts[0], c, acc)

        init = (tuple(scores(c) for c in range(ahead)), jnp.zeros((V_ROWS, n), F32))
        sts, acc = lax.fori_loop(0, n_chunks - ahead, body, init, unroll=unroll)
        for j in range(ahead):
            acc = consume(sts[j], n_chunks - ahead + j, acc)
        _attn_a_finish(acc, o_ref, tq)

    @pl.when(jnp.logical_not(fixed_ok))
    def _():
        def body(c, carry):
            m, acc = carry
            kc, vc = chunk(c)
            st = jnp.dot(kc, q, preferred_element_type=F32)
            m_new = jnp.maximum(m, jnp.max(st, axis=0, keepdims=True))
            p = jnp.exp2(st - m_new).astype(BF16)
            acc = acc * jnp.exp2(m - m_new) + jnp.dot(vc, p, preferred_element_type=F32)
            return m_new, acc

        init = (jnp.full((1, n), NEG_INF, F32), jnp.zeros((V_ROWS, n), F32))
        _, acc = lax.fori_loop(0, n_chunks, body, init)
        _attn_a_finish(acc, o_ref, tq)


def _attn_a_call(qt, k, vt, qn, kn, *, tq=ATTN_A_QUERIES, tk=ATTN_A_KEYS, ahead=1, unroll=True):
    b, _, _, s = qt.shape
    assert s % tq == 0 and s % tk == 0 and s // tk > ahead
    return pl.pallas_call(
        functools.partial(_attn_a_kernel, tk=tk, ahead=ahead, unroll=unroll),
        out_shape=jax.ShapeDtypeStruct((b, s, Q_HEADS * HEAD_DIM), BF16),
        grid=(b, KV_HEADS, s // tq),
        in_specs=[
            pl.BlockSpec(memory_space=pltpu.SMEM),
            pl.BlockSpec((1, GROUP, HEAD_DIM, tq), lambda bi, hk, i: (bi, hk, 0, i)),
            pl.BlockSpec((1, 1, s, HEAD_DIM), lambda bi, hk, i: (bi, hk, 0, 0)),
            pl.BlockSpec((1, 1, V_ROWS, s), lambda bi, hk, i: (bi, hk, 0, 0)),
            pl.BlockSpec((1, GROUP, 1, tq), lambda bi, hk, i: (bi, hk, 0, i)),
        ],
        out_specs=pl.BlockSpec((1, tq, GROUP * HEAD_DIM), lambda bi, hk, i: (bi, i, hk)),
        compiler_params=pltpu.CompilerParams(
            dimension_semantics=("arbitrary", "arbitrary", "arbitrary"),
            vmem_limit_bytes=V7X_VMEM_LIMIT),
        name="attn_global",
    )(_bound_stats(qn, kn), qt, k, vt, qn)


def _attn_b_kernel(sink_ref, stats_ref, bias_ref, q_ref, kl_ref, kc_ref, kr_ref,
                   vl_ref, vc_ref, vr_ref, qn_ref, o_ref, *, nsub):
    hk = pl.program_id(1)
    i = pl.program_id(2)
    last_i = pl.num_programs(2) - 1
    w = kl_ref.shape[2]
    k = jnp.concatenate([kl_ref[0, 0], kc_ref[0, 0], kr_ref[0, 0]], axis=0)
    v = jnp.concatenate([vl_ref[0, 0], vc_ref[0, 0], vr_ref[0, 0]], axis=1)
    sink = jnp.concatenate(
        [jnp.full((1, w), sink_ref[hk * GROUP + hq] * LOG2E, F32) for hq in range(GROUP)],
        axis=1)
    band = bias_ref[1]
    kmax2, fixed_ok = _read_bound_stats(stats_ref)

    def block_lanes(ref, j):
        return jnp.concatenate([ref[0, hq, :, j * w:(j + 1) * w] for hq in range(GROUP)], axis=1)

    def masked_scores(j):
        bias = band
        if j == 0:
            bias = jnp.where(i == 0, bias_ref[0], band)
        if j == nsub - 1:
            bias = jnp.where(i == last_i, bias_ref[2], bias)
        st = jnp.dot(k[j * w:(j + 3) * w], block_lanes(q_ref, j),
                     preferred_element_type=F32)
        return st + jnp.concatenate([bias] * GROUP, axis=1)

    def finish(j, st, m):
        p = jnp.exp2(st - m).astype(BF16)
        acc = jnp.dot(v[:, j * w:(j + 3) * w], p, preferred_element_type=F32)
        denom = acc[HEAD_DIM:HEAD_DIM + 1] + jnp.exp2(sink - m)
        o = acc[:HEAD_DIM] * (1.0 / denom)
        o_ref[0, j * w:(j + 1) * w, :] = jnp.concatenate(
            [o[:, hq * w:(hq + 1) * w].T for hq in range(GROUP)], axis=1).astype(BF16)

    @pl.when(fixed_ok)
    def _():
        ahead = 2
        sts = [masked_scores(j) for j in range(min(ahead, nsub))]
        for j in range(nsub):
            if j + ahead < nsub:
                sts.append(masked_scores(j + ahead))
            bound = jnp.sqrt(block_lanes(qn_ref, j) * kmax2)
            finish(j, sts[j], jnp.maximum(bound, sink))

    @pl.when(jnp.logical_not(fixed_ok))
    def _():
        for j in range(nsub):
            st = masked_scores(j)
            finish(j, st, jnp.maximum(jnp.max(st, axis=0, keepdims=True), sink))


def _band_bias():
    w = WINDOW
    r = np.arange(3 * w)[:, None] - w
    c = np.arange(w)[None, :]
    band = np.abs(r - c) <= w
    first = band & (r >= 0)
    last = band & (r < w)
    table = np.stack([first, band, last]).astype(np.float32)
    return (1.0 - table) * NEG_INF


def _attn_b_call(sink, qt, k, vt, qn, kn, *, nsub=ATTN_B_BLOCKS):
    b, _, _, s = qt.shape
    assert s % (nsub * WINDOW) == 0
    w = WINDOW
    nb = s // w
    span = nsub * w
    bias = jnp.asarray(_band_bias(), F32)
    left = lambda i: jnp.maximum(i * nsub - 1, 0)
    right = lambda i: jnp.minimum((i + 1) * nsub, nb - 1)

    return pl.pallas_call(
        functools.partial(_attn_b_kernel, nsub=nsub),
        out_shape=jax.ShapeDtypeStruct((b, s, Q_HEADS * HEAD_DIM), BF16),
        grid=(b, KV_HEADS, s // span),
        in_specs=[
            pl.BlockSpec(memory_space=pltpu.SMEM),
            pl.BlockSpec(memory_space=pltpu.SMEM),
            pl.BlockSpec((3, 3 * w, w), lambda bi, hk, i: (0, 0, 0)),
            pl.BlockSpec((1, GROUP, HEAD_DIM, span), lambda bi, hk, i: (bi, hk, 0, i)),
            pl.BlockSpec((1, 1, w, HEAD_DIM), lambda bi, hk, i: (bi, hk, left(i), 0)),
            pl.BlockSpec((1, 1, span, HEAD_DIM), lambda bi, hk, i: (bi, hk, i, 0)),
            pl.BlockSpec((1, 1, w, HEAD_DIM), lambda bi, hk, i: (bi, hk, right(i), 0)),
            pl.BlockSpec((1, 1, V_ROWS, w), lambda bi, hk, i: (bi, hk, 0, left(i))),
            pl.BlockSpec((1, 1, V_ROWS, span), lambda bi, hk, i: (bi, hk, 0, i)),
            pl.BlockSpec((1, 1, V_ROWS, w), lambda bi, hk, i: (bi, hk, 0, right(i))),
            pl.BlockSpec((1, GROUP, 1, span), lambda bi, hk, i: (bi, hk, 0, i)),
        ],
        out_specs=pl.BlockSpec((1, span, GROUP * HEAD_DIM), lambda bi, hk, i: (bi, i, hk)),
        compiler_params=pltpu.CompilerParams(
            dimension_semantics=("arbitrary", "arbitrary", "arbitrary"),
            vmem_limit_bytes=V7X_VMEM_LIMIT),
        name="attn_window",
    )(sink, _bound_stats(qn, kn), bias, qt, k, k, k, vt, vt, vt, qn)


def _channel_kernel(x_ref, mod_ref, ya_ref, yb_ref, sg_ref, wb_ref, wo_ref,
                    g2_ref, gf_ref, w1_ref, w2_ref, o_ref, *, n_sub, n_chunks):
    d = x_ref.shape[2]
    ts = x_ref.shape[1] // n_sub
    cw = w1_ref.shape[1] // n_chunks
    dot = functools.partial(jnp.dot, preferred_element_type=F32)
    gate1 = mod_ref[0, 2:3, :]
    shift2 = mod_ref[0, 3:4, :]
    scale2 = mod_ref[0, 4:5, :]
    gate2 = mod_ref[0, 5:6, :]
    for sub in range(n_sub):
        tok = slice(sub * ts, (sub + 1) * ts)
        ua = dot(ya_ref[0, tok, :], wb_ref[0])
        ub = dot(yb_ref[0, tok, :], wb_ref[1])
        sg = sg_ref[0, tok, :]
        merged = sg[:, :d].astype(F32) * ua + sg[:, d:].astype(F32) * ub
        x1 = x_ref[0, tok, :] + gate1 * dot(merged.astype(BF16), wo_ref[...])

        h2 = (_rms(x1) * g2_ref[...] * (1.0 + scale2) + shift2).astype(BF16)
        acc = jnp.zeros((ts, d), F32)
        for c in range(n_chunks):
            hid = dot(h2, w1_ref[:, c * cw:(c + 1) * cw])
            hid = jnp.square(jnp.maximum(hid, 0.0)).astype(BF16)
            acc = acc + dot(hid, w2_ref[c * cw:(c + 1) * cw, :])
        x2 = x1 + gate2 * acc
        o_ref[0, tok, :] = _rms(x2) * gf_ref[...]


def _channel_call(x, mod3, ya, yb, sg, wb, wo, g2, gf, w1, w2, *,
                  tm=CHANNEL_ROWS, n_sub=CHANNEL_SUBTILES, n_chunks=MLP_CHUNKS):
    b, s, d = x.shape
    assert s % tm == 0 and tm % n_sub == 0 and w1.shape[1] % n_chunks == 0
    bw = ya.shape[2]
    ff = w1.shape[1]
    tok = lambda bi, i: (bi, i, 0)
    const2 = lambda bi, i: (0, 0)
    once = pl.Buffered(1)
    return pl.pallas_call(
        functools.partial(_channel_kernel, n_sub=n_sub, n_chunks=n_chunks),
        out_shape=jax.ShapeDtypeStruct((b, s, d), F32),
        grid=(b, s // tm),
        in_specs=[
            pl.BlockSpec((1, tm, d), tok),
            pl.BlockSpec((1, 6, d), lambda bi, i: (bi, 0, 0)),
            pl.BlockSpec((1, tm, bw), tok),
            pl.BlockSpec((1, tm, bw), tok),
            pl.BlockSpec((1, tm, 2 * d), tok),
            pl.BlockSpec((2, bw, d), lambda bi, i: (0, 0, 0), pipeline_mode=once),
            pl.BlockSpec((d, d), const2, pipeline_mode=once),
            pl.BlockSpec((1, d), const2),
            pl.BlockSpec((1, d), const2),
            pl.BlockSpec((d, ff), const2, pipeline_mode=once),
            pl.BlockSpec((ff, d), const2, pipeline_mode=once),
        ],
        out_specs=pl.BlockSpec((1, tm, d), tok),
        compiler_params=pltpu.CompilerParams(
            dimension_semantics=("arbitrary", "arbitrary"),
            vmem_limit_bytes=V7X_VMEM_LIMIT),
        name="merge_mlp_final_norm",
    )(x, mod3, ya, yb, sg, wb, wo, g2, gf, w1, w2)


def _rope_tables(s):
    def cos_sin(pos, dim, theta):
        inv = theta ** (-np.arange(0, dim, 2, dtype=np.float64) / dim)
        ang = pos.astype(np.float64)[:, None] * inv[None, :]
        return np.cos(ang).T, np.sin(ang).T

    t = np.arange(s)
    cr, sr = cos_sin(t // GRID_W, HEAD_DIM // 2, AXIAL_THETA)
    cc, sc = cos_sin(t % GRID_W, HEAD_DIM // 2, AXIAL_THETA)
    c1, s1 = cos_sin(t, HEAD_DIM, ROPE_THETA)
    tables = (np.concatenate([cr, cr, cc, cc]), np.concatenate([-sr, sr, -sc, sc]),
              np.concatenate([c1, c1]), np.concatenate([-s1, s1]))
    return tuple(jnp.asarray(tab.astype(np.float32)) for tab in tables)


def kernel(x, c, w_ada, b_ada, norm1_g, w_in, q_norm_a, k_norm_a, sink_b, w_branch, w_out,
           norm2_g, w_mlp_in, w_mlp_out, final_g):
    b, s, d = x.shape
    assert w_ada.shape[0] == 1, "kernel supports DEPTH == 1"
    l = 0
    cos_a, sin_a, cos_b, sin_b = _rope_tables(s)

    mod3 = _mod_call(c, w_ada[l], b_ada[l][None, :]).reshape(b, 6, d)
    qa, ka, va, qna, kna, qb, kb, vb, qnb, knb, sg = _proj_call(
        x, mod3, norm1_g[l][None, :], w_in[l].astype(BF16),
        q_norm_a[l][:, None], k_norm_a[l][:, None], cos_a, sin_a, cos_b, sin_b)
    ya = _attn_a_call(qa, ka, va, qna, kna)
    yb = _attn_b_call(sink_b[l], qb, kb, vb, qnb, knb)
    return _channel_call(x, mod3, ya, yb, sg, w_branch[l].astype(BF16), w_out[l].astype(BF16),
                         norm2_g[l][None, :], final_g[None, :],
                         w_mlp_in[l].astype(BF16), w_mlp_out[l].astype(BF16))
```

```python
import functools
import math

import jax
import jax.numpy as jnp
import numpy as np
from jax import lax
from jax.experimental import pallas as pl
from jax.experimental.pallas import tpu as pltpu

F32 = jnp.float32
BF16 = jnp.bfloat16

HEAD_DIM = 64
Q_HEADS = 8
KV_HEADS = 2
GROUP = Q_HEADS // KV_HEADS
GRID_W = 64
WINDOW = 128
ROPE_THETA = 10000.0
AXIAL_THETA = 10000.0
NORM_EPS = 1e-6
NEG_INF = -1e30
LOG2E = math.log2(math.e)

V7X_BF16_SUBLANES = 16
V7X_VMEM_LIMIT = 56 * 1024 * 1024

MOD_COLS = 2048
PROJ_ROWS, PROJ_SUBTILES = 1024, 4
ATTN_A_QUERIES, ATTN_A_KEYS = 256, 256
ATTN_B_BLOCKS = 16
CHANNEL_ROWS, CHANNEL_SUBTILES, MLP_CHUNKS = 1024, 2, 4

V_ROWS = HEAD_DIM + V7X_BF16_SUBLANES

_BF16_SQUARE_MARGIN = (1.0 + 2.0 ** -8) ** 2


def _rms(x, eps=NORM_EPS):
    return x * lax.rsqrt(jnp.mean(x * x, axis=-1, keepdims=True) + eps)


def _mod_kernel(c_ref, w_ref, b_ref, o_ref):
    c = c_ref[...]
    a = c * (1.0 / (1.0 + jnp.exp(-c)))
    w = w_ref[...]
    a_hi = a.astype(BF16)
    a_lo = (a - a_hi.astype(F32)).astype(BF16)
    w_hi = w.astype(BF16)
    w_lo = (w - w_hi.astype(F32)).astype(BF16)
    dot = functools.partial(jnp.dot, preferred_element_type=F32)
    o_ref[...] = dot(a_hi, w_hi) + (dot(a_hi, w_lo) + dot(a_lo, w_hi)) + b_ref[...]


def _mod_call(c, w_ada, b_ada, *, tn=MOD_COLS):
    rows, d = c.shape
    n = w_ada.shape[1]
    return pl.pallas_call(
        _mod_kernel,
        out_shape=jax.ShapeDtypeStruct((rows, n), F32),
        grid=(n // tn,),
        in_specs=[
            pl.BlockSpec((rows, d), lambda j: (0, 0)),
            pl.BlockSpec((d, tn), lambda j: (0, j)),
            pl.BlockSpec((1, tn), lambda j: (0, j)),
        ],
        out_specs=pl.BlockSpec((rows, tn), lambda j: (0, j)),
        compiler_params=pltpu.CompilerParams(
            dimension_semantics=("arbitrary",), vmem_limit_bytes=V7X_VMEM_LIMIT),
        name="adaln_mod",
    )(c, w_ada, b_ada)


def _rope_t(x, cos, sin, half):
    parts = []
    for base in range(0, HEAD_DIM, 2 * half):
        parts.append(x[base + half:base + 2 * half])
        parts.append(x[base:base + half])
    x_rot = jnp.concatenate(parts, axis=0)
    return x * cos + x_rot * sin


def _proj_kernel(x_ref, mod_ref, g1_ref, w_ref, gq_ref, gk_ref,
                 cos_a_ref, sin_a_ref, cos_b_ref, sin_b_ref,
                 qa_ref, ka_ref, va_ref, qna_ref, kna_ref,
                 qb_ref, kb_ref, vb_ref, qnb_ref, knb_ref, sg_ref, wt_ref, *, n_sub, n_qkv):
    @pl.when((pl.program_id(0) == 0) & (pl.program_id(1) == 0))
    def _():
        wt_ref[...] = w_ref[:, :n_qkv].T

    shift1 = mod_ref[0, 0:1, :]
    scale1 = mod_ref[0, 1:2, :]
    g1 = g1_ref[...]
    gq, gk = gq_ref[...], gk_ref[...]
    q_scale = HEAD_DIM ** -0.5 * LOG2E
    ts = x_ref.shape[1] // n_sub
    ones = jnp.ones((V_ROWS - HEAD_DIM, ts), BF16)

    def normed(t, g):
        ms = jnp.sum(t * t, axis=0, keepdims=True) * (1.0 / HEAD_DIM)
        return t * lax.rsqrt(ms + NORM_EPS) * g

    def norm2_bound(t):
        return jnp.sum(t * t, axis=0, keepdims=True) * _BF16_SQUARE_MARGIN

    for sub in range(n_sub):
        tok = slice(sub * ts, (sub + 1) * ts)
        x = x_ref[0, tok, :]
        h = (_rms(x) * g1 * (1.0 + scale1) + shift1).astype(BF16)

        pt = lax.dot_general(wt_ref[...], h, (((1,), (1,)), ((), ())),
                             preferred_element_type=F32)
        cos_a, sin_a = cos_a_ref[:, tok], sin_a_ref[:, tok]
        cos_b, sin_b = cos_b_ref[:, tok], sin_b_ref[:, tok]

        def head(row):
            return pt[row:row + HEAD_DIM]

        row = 0
        for hq in range(Q_HEADS):
            t = _rope_t(normed(head(row), gq), cos_a, sin_a, HEAD_DIM // 4) * q_scale
            qa_ref[0, hq, :, tok] = t.astype(BF16)
            qna_ref[0, hq, :, tok] = norm2_bound(t)
            row += HEAD_DIM
        for hk in range(KV_HEADS):
            t = _rope_t(normed(head(row), gk), cos_a, sin_a, HEAD_DIM // 4)
            ka_ref[0, hk, tok, :] = t.T.astype(BF16)
            kna_ref[0, hk, :, tok] = norm2_bound(t)
            row += HEAD_DIM
        for hk in range(KV_HEADS):
            va_ref[0, hk, :, tok] = jnp.concatenate([head(row).astype(BF16), ones], axis=0)
            row += HEAD_DIM
        for hq in range(Q_HEADS):
            t = _rope_t(head(row), cos_b, sin_b, HEAD_DIM // 2) * q_scale
            qb_ref[0, hq, :, tok] = t.astype(BF16)
            qnb_ref[0, hq, :, tok] = norm2_bound(t)
            row += HEAD_DIM
        for hk in range(KV_HEADS):
            t = _rope_t(head(row), cos_b, sin_b, HEAD_DIM // 2)
            kb_ref[0, hk, tok, :] = t.T.astype(BF16)
            knb_ref[0, hk, :, tok] = norm2_bound(t)
            row += HEAD_DIM
        for hk in range(KV_HEADS):
            vb_ref[0, hk, :, tok] = jnp.concatenate([head(row).astype(BF16), ones], axis=0)
            row += HEAD_DIM

        g = jnp.dot(h, w_ref[:, n_qkv:], preferred_element_type=F32)
        sg_ref[0, tok, :] = (0.5 * jnp.tanh(0.5 * g) + 0.5).astype(BF16)


def _proj_call(x, mod3, g1, w, gq, gk, cos_a, sin_a, cos_b, sin_b, *,
               tm=PROJ_ROWS, n_sub=PROJ_SUBTILES):
    b, s, d = x.shape
    assert s % tm == 0 and tm % n_sub == 0
    n_qkv = 2 * (Q_HEADS + 2 * KV_HEADS) * HEAD_DIM
    n_g = w.shape[1] - n_qkv
    const2 = lambda bi, i: (0, 0)
    tab = pl.BlockSpec((HEAD_DIM, tm), lambda bi, i: (0, i))
    q_shape = jax.ShapeDtypeStruct((b, Q_HEADS, HEAD_DIM, s), BF16)
    k_shape = jax.ShapeDtypeStruct((b, KV_HEADS, s, HEAD_DIM), BF16)
    v_shape = jax.ShapeDtypeStruct((b, KV_HEADS, V_ROWS, s), BF16)
    q_spec = pl.BlockSpec((1, Q_HEADS, HEAD_DIM, tm), lambda bi, i: (bi, 0, 0, i))
    k_spec = pl.BlockSpec((1, KV_HEADS, tm, HEAD_DIM), lambda bi, i: (bi, 0, i, 0))
    v_spec = pl.BlockSpec((1, KV_HEADS, V_ROWS, tm), lambda bi, i: (bi, 0, 0, i))
    qn_shape = jax.ShapeDtypeStruct((b, Q_HEADS, 1, s), F32)
    qn_spec = pl.BlockSpec((1, Q_HEADS, 1, tm), lambda bi, i: (bi, 0, 0, i))
    kn_shape = jax.ShapeDtypeStruct((b, KV_HEADS, 1, s), F32)
    kn_spec = pl.BlockSpec((1, KV_HEADS, 1, tm), lambda bi, i: (bi, 0, 0, i))
    mixer_shapes = (q_shape, k_shape, v_shape, qn_shape, kn_shape)
    mixer_specs = (q_spec, k_spec, v_spec, qn_spec, kn_spec)
    return pl.pallas_call(
        functools.partial(_proj_kernel, n_sub=n_sub, n_qkv=n_qkv),
        out_shape=mixer_shapes + mixer_shapes + (jax.ShapeDtypeStruct((b, s, n_g), BF16),),
        grid=(b, s // tm),
        in_specs=[
            pl.BlockSpec((1, tm, d), lambda bi, i: (bi, i, 0)),
            pl.BlockSpec((1, 6, d), lambda bi, i: (bi, 0, 0)),
            pl.BlockSpec((1, d), const2),
            pl.BlockSpec((d, n_qkv + n_g), const2, pipeline_mode=pl.Buffered(1)),
            pl.BlockSpec((HEAD_DIM, 1), const2),
            pl.BlockSpec((HEAD_DIM, 1), const2),
            tab, tab, tab, tab,
        ],
        out_specs=mixer_specs + mixer_specs + (
            pl.BlockSpec((1, tm, n_g), lambda bi, i: (bi, i, 0)),),
        scratch_shapes=[pltpu.VMEM((n_qkv, d), BF16)],
        compiler_params=pltpu.CompilerParams(
            dimension_semantics=("arbitrary", "arbitrary"),
            vmem_limit_bytes=V7X_VMEM_LIMIT),
        name="in_proj",
    )(x, mod3, g1, w, gq, gk, cos_a, sin_a, cos_b, sin_b)


_MAX_FIXED_STABILISER = 50.0


def _attn_a_finish(acc, o_ref, tq):
    o = acc[:HEAD_DIM] * (1.0 / acc[HEAD_DIM:HEAD_DIM + 1])
    o_ref[0] = jnp.concatenate(
        [o[:, hq * tq:(hq + 1) * tq].T for hq in range(GROUP)], axis=1).astype(BF16)


def _bound_stats(qn, kn):
    b = qn.shape[0]
    kmax2 = jnp.max(kn, axis=(2, 3))
    qmax2 = jnp.max(qn.reshape(b, KV_HEADS, -1), axis=2)
    return jnp.stack([kmax2, qmax2 * kmax2], axis=-1).reshape(-1)


def _read_bound_stats(stats_ref):
    base = (pl.program_id(0) * KV_HEADS + pl.program_id(1)) * 2
    return stats_ref[base], stats_ref[base + 1] <= _MAX_FIXED_STABILISER ** 2


def _attn_a_kernel(stats_ref, q_ref, k_ref, v_ref, qn_ref, o_ref, *, tk, ahead, unroll):
    tq = q_ref.shape[3]
    n = GROUP * tq
    n_chunks = k_ref.shape[2] // tk
    q = jnp.concatenate([q_ref[0, hq] for hq in range(GROUP)], axis=1)
    kmax2, fixed_ok = _read_bound_stats(stats_ref)

    def chunk(c):
        start = pl.multiple_of(c * tk, tk)
        return k_ref[0, 0, pl.ds(start, tk), :], v_ref[0, 0, :, pl.ds(start, tk)]

    @pl.when(fixed_ok)
    def _():
        qn2 = jnp.concatenate([qn_ref[0, hq] for hq in range(GROUP)], axis=1)
        bound = jnp.sqrt(qn2 * kmax2)

        def scores(c):
            return jnp.dot(chunk(c)[0], q, preferred_element_type=F32)

        def consume(st, c, acc):
            p = jnp.exp2(st - bound).astype(BF16)
            return acc + jnp.dot(chunk(c)[1], p, preferred_element_type=F32)

        def body(c, carry):
            sts, acc = carry
            return sts[1:] + (scores(c + ahead),), consume(sts[0], c, acc)

        init = (tuple(scores(c) for c in range(ahead)), jnp.zeros((V_ROWS, n), F32))
        sts, acc = lax.fori_loop(0, n_chunks - ahead, body, init, unroll=unroll)
        for j in range(ahead):
            acc = consume(sts[j], n_chunks - ahead + j, acc)
        _attn_a_finish(acc, o_ref, tq)

    @pl.when(jnp.logical_not(fixed_ok))
    def _():
        def body(c, carry):
            m, acc = carry
            kc, vc = chunk(c)
            st = jnp.dot(kc, q, preferred_element_type=F32)
            m_new = jnp.maximum(m, jnp.max(st, axis=0, keepdims=True))
            p = jnp.exp2(st - m_new).astype(BF16)
            acc = acc * jnp.exp2(m - m_new) + jnp.dot(vc, p, preferred_element_type=F32)
            return m_new, acc

        init = (jnp.full((1, n), NEG_INF, F32), jnp.zeros((V_ROWS, n), F32))
        _, acc = lax.fori_loop(0, n_chunks, body, init)
        _attn_a_finish(acc, o_ref, tq)


def _attn_a_call(qt, k, vt, qn, kn, *, tq=ATTN_A_QUERIES, tk=ATTN_A_KEYS, ahead=1, unroll=True):
    b, _, _, s = qt.shape
    assert s % tq == 0 and s % tk == 0 and s // tk > ahead
    return pl.pallas_call(
        functools.partial(_attn_a_kernel, tk=tk, ahead=ahead, unroll=unroll),
        out_shape=jax.ShapeDtypeStruct((b, s, Q_HEADS * HEAD_DIM), BF16),
        grid=(b, KV_HEADS, s // tq),
        in_specs=[
            pl.BlockSpec(memory_space=pltpu.SMEM),
            pl.BlockSpec((1, GROUP, HEAD_DIM, tq), lambda bi, hk, i: (bi, hk, 0, i)),
            pl.BlockSpec((1, 1, s, HEAD_DIM), lambda bi, hk, i: (bi, hk, 0, 0)),
            pl.BlockSpec((1, 1, V_ROWS, s), lambda bi, hk, i: (bi, hk, 0, 0)),
            pl.BlockSpec((1, GROUP, 1, tq), lambda bi, hk, i: (bi, hk, 0, i)),
        ],
        out_specs=pl.BlockSpec((1, tq, GROUP * HEAD_DIM), lambda bi, hk, i: (bi, i, hk)),
        compiler_params=pltpu.CompilerParams(
            dimension_semantics=("arbitrary", "arbitrary", "arbitrary"),
            vmem_limit_bytes=V7X_VMEM_LIMIT),
        name="attn_global",
    )(_bound_stats(qn, kn), qt, k, vt, qn)


def _attn_b_kernel(sink_ref, stats_ref, bias_ref, q_ref, kl_ref, kc_ref, kr_ref,
                   vl_ref, vc_ref, vr_ref, qn_ref, o_ref, *, nsub):
    hk = pl.program_id(1)
    i = pl.program_id(2)
    last_i = pl.num_programs(2) - 1
    w = kl_ref.shape[2]
    k = jnp.concatenate([kl_ref[0, 0], kc_ref[0, 0], kr_ref[0, 0]], axis=0)
    v = jnp.concatenate([vl_ref[0, 0], vc_ref[0, 0], vr_ref[0, 0]], axis=1)
    sink = jnp.concatenate(
        [jnp.full((1, w), sink_ref[hk * GROUP + hq] * LOG2E, F32) for hq in range(GROUP)],
        axis=1)
    band = bias_ref[1]
    kmax2, fixed_ok = _read_bound_stats(stats_ref)

    def block_lanes(ref, j):
        return jnp.concatenate([ref[0, hq, :, j * w:(j + 1) * w] for hq in range(GROUP)], axis=1)

    def masked_scores(j):
        bias = band
        if j == 0:
            bias = jnp.where(i == 0, bias_ref[0], band)
        if j == nsub - 1:
            bias = jnp.where(i == last_i, bias_ref[2], bias)
        st = jnp.dot(k[j * w:(j + 3) * w], block_lanes(q_ref, j),
                     preferred_element_type=F32)
        return st + jnp.concatenate([bias] * GROUP, axis=1)

    def finish(j, st, m):
        p = jnp.exp2(st - m).astype(BF16)
        acc = jnp.dot(v[:, j * w:(j + 3) * w], p, preferred_element_type=F32)
        denom = acc[HEAD_DIM:HEAD_DIM + 1] + jnp.exp2(sink - m)
        o = acc[:HEAD_DIM] * (1.0 / denom)
        o_ref[0, j * w:(j + 1) * w, :] = jnp.concatenate(
            [o[:, hq * w:(hq + 1) * w].T for hq in range(GROUP)], axis=1).astype(BF16)

    @pl.when(fixed_ok)
    def _():
        ahead = 2
        sts = [masked_scores(j) for j in range(min(ahead, nsub))]
        for j in range(nsub):
            if j + ahead < nsub:
                sts.append(masked_scores(j + ahead))
            bound = jnp.sqrt(block_lanes(qn_ref, j) * kmax2)
            finish(j, sts[j], jnp.maximum(bound, sink))

    @pl.when(jnp.logical_not(fixed_ok))
    def _():
        for j in range(nsub):
            st = masked_scores(j)
            finish(j, st, jnp.maximum(jnp.max(st, axis=0, keepdims=True), sink))


def _band_bias():
    w = WINDOW
    r = np.arange(3 * w)[:, None] - w
    c = np.arange(w)[None, :]
    band = np.abs(r - c) <= w
    first = band & (r >= 0)
    last = band & (r < w)
    table = np.stack([first, band, last]).astype(np.float32)
    return (1.0 - table) * NEG_INF


def _attn_b_call(sink, qt, k, vt, qn, kn, *, nsub=ATTN_B_BLOCKS):
    b, _, _, s = qt.shape
    assert s % (nsub * WINDOW) == 0
    w = WINDOW
    nb = s // w
    span = nsub * w
    bias = jnp.asarray(_band_bias(), F32)
    left = lambda i: jnp.maximum(i * nsub - 1, 0)
    right = lambda i: jnp.minimum((i + 1) * nsub, nb - 1)

    return pl.pallas_call(
        functools.partial(_attn_b_kernel, nsub=nsub),
        out_shape=jax.ShapeDtypeStruct((b, s, Q_HEADS * HEAD_DIM), BF16),
        grid=(b, KV_HEADS, s // span),
        in_specs=[
            pl.BlockSpec(memory_space=pltpu.SMEM),
            pl.BlockSpec(memory_space=pltpu.SMEM),
            pl.BlockSpec((3, 3 * w, w), lambda bi, hk, i: (0, 0, 0)),
            pl.BlockSpec((1, GROUP, HEAD_DIM, span), lambda bi, hk, i: (bi, hk, 0, i)),
            pl.BlockSpec((1, 1, w, HEAD_DIM), lambda bi, hk, i: (bi, hk, left(i), 0)),
            pl.BlockSpec((1, 1, span, HEAD_DIM), lambda bi, hk, i: (bi, hk, i, 0)),
            pl.BlockSpec((1, 1, w, HEAD_DIM), lambda bi, hk, i: (bi, hk, right(i), 0)),
            pl.BlockSpec((1, 1, V_ROWS, w), lambda bi, hk, i: (bi, hk, 0, left(i))),
            pl.BlockSpec((1, 1, V_ROWS, span), lambda bi, hk, i: (bi, hk, 0, i)),
            pl.BlockSpec((1, 1, V_ROWS, w), lambda bi, hk, i: (bi, hk, 0, right(i))),
            pl.BlockSpec((1, GROUP, 1, span), lambda bi, hk, i: (bi, hk, 0, i)),
        ],
        out_specs=pl.BlockSpec((1, span, GROUP * HEAD_DIM), lambda bi, hk, i: (bi, i, hk)),
        compiler_params=pltpu.CompilerParams(
            dimension_semantics=("arbitrary", "arbitrary", "arbitrary"),
            vmem_limit_bytes=V7X_VMEM_LIMIT),
        name="attn_window",
    )(sink, _bound_stats(qn, kn), bias, qt, k, k, k, vt, vt, vt, qn)


def _channel_kernel(x_ref, mod_ref, ya_ref, yb_ref, sg_ref, wb_ref, wo_ref,
                    g2_ref, gf_ref, w1_ref, w2_ref, o_ref, *, n_sub, n_chunks):
    d = x_ref.shape[2]
    ts = x_ref.shape[1] // n_sub
    cw = w1_ref.shape[1] // n_chunks
    dot = functools.partial(jnp.dot, preferred_element_type=F32)
    gate1 = mod_ref[0, 2:3, :]
    shift2 = mod_ref[0, 3:4, :]
    scale2 = mod_ref[0, 4:5, :]
    gate2 = mod_ref[0, 5:6, :]
    for sub in range(n_sub):
        tok = slice(sub * ts, (sub + 1) * ts)
        ua = dot(ya_ref[0, tok, :], wb_ref[0])
        ub = dot(yb_ref[0, tok, :], wb_ref[1])
        sg = sg_ref[0, tok, :]
        merged = sg[:, :d].astype(F32) * ua + sg[:, d:].astype(F32) * ub
        x1 = x_ref[0, tok, :] + gate1 * dot(merged.astype(BF16), wo_ref[...])

        h2 = (_rms(x1) * g2_ref[...] * (1.0 + scale2) + shift2).astype(BF16)
        acc = jnp.zeros((ts, d), F32)
        for c in range(n_chunks):
            hid = dot(h2, w1_ref[:, c * cw:(c + 1) * cw])
            hid = jnp.square(jnp.maximum(hid, 0.0)).astype(BF16)
            acc = acc + dot(hid, w2_ref[c * cw:(c + 1) * cw, :])
        x2 = x1 + gate2 * acc
        o_ref[0, tok, :] = _rms(x2) * gf_ref[...]


def _channel_call(x, mod3, ya, yb, sg, wb, wo, g2, gf, w1, w2, *,
                  tm=CHANNEL_ROWS, n_sub=CHANNEL_SUBTILES, n_chunks=MLP_CHUNKS):
    b, s, d = x.shape
    assert s % tm == 0 and tm % n_sub == 0 and w1.shape[1] % n_chunks == 0
    bw = ya.shape[2]
    ff = w1.shape[1]
    tok = lambda bi, i: (bi, i, 0)
    const2 = lambda bi, i: (0, 0)
    once = pl.Buffered(1)
    return pl.pallas_call(
        functools.partial(_channel_kernel, n_sub=n_sub, n_chunks=n_chunks),
        out_shape=jax.ShapeDtypeStruct((b, s, d), F32),
        grid=(b, s // tm),
        in_specs=[
            pl.BlockSpec((1, tm, d), tok),
            pl.BlockSpec((1, 6, d), lambda bi, i: (bi, 0, 0)),
            pl.BlockSpec((1, tm, bw), tok),
            pl.BlockSpec((1, tm, bw), tok),
            pl.BlockSpec((1, tm, 2 * d), tok),
            pl.BlockSpec((2, bw, d), lambda bi, i: (0, 0, 0), pipeline_mode=once),
            pl.BlockSpec((d, d), const2, pipeline_mode=once),
            pl.BlockSpec((1, d), const2),
            pl.BlockSpec((1, d), const2),
            pl.BlockSpec((d, ff), const2, pipeline_mode=once),
            pl.BlockSpec((ff, d), const2, pipeline_mode=once),
        ],
        out_specs=pl.BlockSpec((1, tm, d), tok),
        compiler_params=pltpu.CompilerParams(
            dimension_semantics=("arbitrary", "arbitrary"),
            vmem_limit_bytes=V7X_VMEM_LIMIT),
        name="merge_mlp_final_norm",
    )(x, mod3, ya, yb, sg, wb, wo, g2, gf, w1, w2)


def _rope_tables(s):
    def cos_sin(pos, dim, theta):
        inv = theta ** (-np.arange(0, dim, 2, dtype=np.float64) / dim)
        ang = pos.astype(np.float64)[:, None] * inv[None, :]
        return np.cos(ang).T, np.sin(ang).T

    t = np.arange(s)
    cr, sr = cos_sin(t // GRID_W, HEAD_DIM // 2, AXIAL_THETA)
    cc, sc = cos_sin(t % GRID_W, HEAD_DIM // 2, AXIAL_THETA)
    c1, s1 = cos_sin(t, HEAD_DIM, ROPE_THETA)
    tables = (np.concatenate([cr, cr, cc, cc]), np.concatenate([-sr, sr, -sc, sc]),
              np.concatenate([c1, c1]), np.concatenate([-s1, s1]))
    return tuple(jnp.asarray(tab.astype(np.float32)) for tab in tables)


def kernel(x, c, w_ada, b_ada, norm1_g, w_in, q_norm_a, k_norm_a, sink_b, w_branch, w_out,
           norm2_g, w_mlp_in, w_mlp_out, final_g):
    b, s, d = x.shape
    assert w_ada.shape[0] == 1, "kernel supports DEPTH == 1"
    l = 0
    cos_a, sin_a, cos_b, sin_b = _rope_tables(s)

    mod3 = _mod_call(c, w_ada[l], b_ada[l][None, :]).reshape(b, 6, d)
    qa, ka, va, qna, kna, qb, kb, vb, qnb, knb, sg = _proj_call(
        x, mod3, norm1_g[l][None, :], w_in[l].astype(BF16),
        q_norm_a[l][:, None], k_norm_a[l][:, None], cos_a, sin_a, cos_b, sin_b)
    ya = _attn_a_call(qa, ka, va, qna, kna)
    yb = _attn_b_call(sink_b[l], qb, kb, vb, qnb, knb)
    return _channel_call(x, mod3, ya, yb, sg, w_branch[l].astype(BF16), w_out[l].astype(BF16),
                         norm2_g[l][None, :], final_g[None, :],
                         w_mlp_in[l].astype(BF16), w_mlp_out[l].astype(BF16))
```

```python
import functools
import math

import jax
import jax.numpy as jnp
import numpy as np
from jax import lax
from jax.experimental import pallas as pl
from jax.experimental.pallas import tpu as pltpu

F32 = jnp.float32
BF16 = jnp.bfloat16

HEAD_DIM = 64
Q_HEADS = 8
KV_HEADS = 2
GROUP = Q_HEADS // KV_HEADS
GRID_W = 64
WINDOW = 128
ROPE_THETA = 10000.0
AXIAL_THETA = 10000.0
NORM_EPS = 1e-6
NEG_INF = -1e30
LOG2E = math.log2(math.e)

V7X_BF16_SUBLANES = 16
V7X_VMEM_LIMIT = 56 * 1024 * 1024

MOD_COLS = 2048
PROJ_ROWS, PROJ_SUBTILES = 1024, 4
ATTN_A_QUERIES, ATTN_A_KEYS = 256, 256
ATTN_B_BLOCKS = 16
CHANNEL_ROWS, CHANNEL_SUBTILES, MLP_CHUNKS = 1024, 2, 4

V_ROWS_A = 128
V_ROWS_B = HEAD_DIM + V7X_BF16_SUBLANES

_BF16_SQUARE_MARGIN = (1.0 + 2.0 ** -8) ** 2


def _rms(x, eps=NORM_EPS):
    return x * lax.rsqrt(jnp.mean(x * x, axis=-1, keepdims=True) + eps)


def _mod_kernel(c_ref, w_ref, b_ref, o_ref):
    c = c_ref[...]
    a = c * (1.0 / (1.0 + jnp.exp(-c)))
    w = w_ref[...]
    a_hi = a.astype(BF16)
    a_lo = (a - a_hi.astype(F32)).astype(BF16)
    w_hi = w.astype(BF16)
    w_lo = (w - w_hi.astype(F32)).astype(BF16)
    dot = functools.partial(jnp.dot, preferred_element_type=F32)
    o_ref[...] = dot(a_hi, w_hi) + (dot(a_hi, w_lo) + dot(a_lo, w_hi)) + b_ref[...]


def _mod_call(c, w_ada, b_ada, *, tn=MOD_COLS):
    rows, d = c.shape
    n = w_ada.shape[1]
    return pl.pallas_call(
        _mod_kernel,
        out_shape=jax.ShapeDtypeStruct((rows, n), F32),
        grid=(n // tn,),
        in_specs=[
            pl.BlockSpec((rows, d), lambda j: (0, 0)),
            pl.BlockSpec((d, tn), lambda j: (0, j)),
            pl.BlockSpec((1, tn), lambda j: (0, j)),
        ],
        out_specs=pl.BlockSpec((rows, tn), lambda j: (0, j)),
        compiler_params=pltpu.CompilerParams(
            dimension_semantics=("arbitrary",), vmem_limit_bytes=V7X_VMEM_LIMIT),
        name="adaln_mod",
    )(c, w_ada, b_ada)


def _rope_t(x, cos, sin, half):
    parts = []
    for base in range(0, HEAD_DIM, 2 * half):
        parts.append(x[base + half:base + 2 * half])
        parts.append(x[base:base + half])
    x_rot = jnp.concatenate(parts, axis=0)
    return x * cos + x_rot * sin


def _proj_kernel(x_ref, mod_ref, g1_ref, w_ref, gq_ref, gk_ref,
                 cos_a_ref, sin_a_ref, cos_b_ref, sin_b_ref,
                 qa_ref, ka_ref, va_ref, qna_ref, kna_ref,
                 qb_ref, kb_ref, vb_ref, qnb_ref, knb_ref, sg_ref, wt_ref, *, n_sub, n_qkv):
    @pl.when((pl.program_id(0) == 0) & (pl.program_id(1) == 0))
    def _():
        wt_ref[...] = w_ref[:, :n_qkv].T

    shift1 = mod_ref[0, 0:1, :]
    scale1 = mod_ref[0, 1:2, :]
    g1 = g1_ref[...]
    gq, gk = gq_ref[...], gk_ref[...]
    q_scale = HEAD_DIM ** -0.5 * LOG2E
    ts = x_ref.shape[1] // n_sub
    ones_a = jnp.ones((va_ref.shape[2] - HEAD_DIM, ts), BF16)
    ones_b = jnp.ones((vb_ref.shape[2] - HEAD_DIM, ts), BF16)

    def normed(t, g):
        ms = jnp.sum(t * t, axis=0, keepdims=True) * (1.0 / HEAD_DIM)
        return t * lax.rsqrt(ms + NORM_EPS) * g

    def norm2_bound(t):
        return jnp.sum(t * t, axis=0, keepdims=True) * _BF16_SQUARE_MARGIN

    for sub in range(n_sub):
        tok = slice(sub * ts, (sub + 1) * ts)
        x = x_ref[0, tok, :]
        h = (_rms(x) * g1 * (1.0 + scale1) + shift1).astype(BF16)

        pt = lax.dot_general(wt_ref[...], h, (((1,), (1,)), ((), ())),
                             preferred_element_type=F32)
        cos_a, sin_a = cos_a_ref[:, tok], sin_a_ref[:, tok]
        cos_b, sin_b = cos_b_ref[:, tok], sin_b_ref[:, tok]

        def head(row):
            return pt[row:row + HEAD_DIM]

        row = 0
        for hq in range(Q_HEADS):
            t = _rope_t(normed(head(row), gq), cos_a, sin_a, HEAD_DIM // 4) * q_scale
            qa_ref[0, hq, :, tok] = t.astype(BF16)
            qna_ref[0, hq, :, tok] = norm2_bound(t)
            row += HEAD_DIM
        for hk in range(KV_HEADS):
            t = _rope_t(normed(head(row), gk), cos_a, sin_a, HEAD_DIM // 4)
            ka_ref[0, hk, tok, :] = t.T.astype(BF16)
            kna_ref[0, hk, :, tok] = norm2_bound(t)
            row += HEAD_DIM
        for hk in range(KV_HEADS):
            va_ref[0, hk, :, tok] = jnp.concatenate([head(row).astype(BF16), ones_a], axis=0)
            row += HEAD_DIM
        for hq in range(Q_HEADS):
            t = _rope_t(head(row), cos_b, sin_b, HEAD_DIM // 2) * q_scale
            qb_ref[0, hq, :, tok] = t.astype(BF16)
            qnb_ref[0, hq, :, tok] = norm2_bound(t)
            row += HEAD_DIM
        for hk in range(KV_HEADS):
            t = _rope_t(head(row), cos_b, sin_b, HEAD_DIM // 2)
            kb_ref[0, hk, tok, :] = t.T.astype(BF16)
            knb_ref[0, hk, :, tok] = norm2_bound(t)
            row += HEAD_DIM
        for hk in range(KV_HEADS):
            vb_ref[0, hk, :, tok] = jnp.concatenate([head(row).astype(BF16), ones_b], axis=0)
            row += HEAD_DIM

        g = jnp.dot(h, w_ref[:, n_qkv:], preferred_element_type=F32)
        sg_ref[0, tok, :] = (0.5 * jnp.tanh(0.5 * g) + 0.5).astype(BF16)


def _proj_call(x, mod3, g1, w, gq, gk, cos_a, sin_a, cos_b, sin_b, *,
               tm=PROJ_ROWS, n_sub=PROJ_SUBTILES):
    b, s, d = x.shape
    assert s % tm == 0 and tm % n_sub == 0
    n_qkv = 2 * (Q_HEADS + 2 * KV_HEADS) * HEAD_DIM
    n_g = w.shape[1] - n_qkv
    const2 = lambda bi, i: (0, 0)
    tab = pl.BlockSpec((HEAD_DIM, tm), lambda bi, i: (0, i))
    q_shape = jax.ShapeDtypeStruct((b, Q_HEADS, HEAD_DIM, s), BF16)
    k_shape = jax.ShapeDtypeStruct((b, KV_HEADS, s, HEAD_DIM), BF16)
    q_spec = pl.BlockSpec((1, Q_HEADS, HEAD_DIM, tm), lambda bi, i: (bi, 0, 0, i))
    k_spec = pl.BlockSpec((1, KV_HEADS, tm, HEAD_DIM), lambda bi, i: (bi, 0, i, 0))

    def v_out(rows):
        return (jax.ShapeDtypeStruct((b, KV_HEADS, rows, s), BF16),
                pl.BlockSpec((1, KV_HEADS, rows, tm), lambda bi, i: (bi, 0, 0, i)))

    (va_shape, va_spec), (vb_shape, vb_spec) = v_out(V_ROWS_A), v_out(V_ROWS_B)
    qn_shape = jax.ShapeDtypeStruct((b, Q_HEADS, 1, s), F32)
    qn_spec = pl.BlockSpec((1, Q_HEADS, 1, tm), lambda bi, i: (bi, 0, 0, i))
    kn_shape = jax.ShapeDtypeStruct((b, KV_HEADS, 1, s), F32)
    kn_spec = pl.BlockSpec((1, KV_HEADS, 1, tm), lambda bi, i: (bi, 0, 0, i))
    a_shapes = (q_shape, k_shape, va_shape, qn_shape, kn_shape)
    a_specs = (q_spec, k_spec, va_spec, qn_spec, kn_spec)
    b_shapes = (q_shape, k_shape, vb_shape, qn_shape, kn_shape)
    b_specs = (q_spec, k_spec, vb_spec, qn_spec, kn_spec)
    return pl.pallas_call(
        functools.partial(_proj_kernel, n_sub=n_sub, n_qkv=n_qkv),
        out_shape=a_shapes + b_shapes + (jax.ShapeDtypeStruct((b, s, n_g), BF16),),
        grid=(b, s // tm),
        in_specs=[
            pl.BlockSpec((1, tm, d), lambda bi, i: (bi, i, 0)),
            pl.BlockSpec((1, 6, d), lambda bi, i: (bi, 0, 0)),
            pl.BlockSpec((1, d), const2),
            pl.BlockSpec((d, n_qkv + n_g), const2, pipeline_mode=pl.Buffered(1)),
            pl.BlockSpec((HEAD_DIM, 1), const2),
            pl.BlockSpec((HEAD_DIM, 1), const2),
            tab, tab, tab, tab,
        ],
        out_specs=a_specs + b_specs + (
            pl.BlockSpec((1, tm, n_g), lambda bi, i: (bi, i, 0)),),
        scratch_shapes=[pltpu.VMEM((n_qkv, d), BF16)],
        compiler_params=pltpu.CompilerParams(
            dimension_semantics=("arbitrary", "arbitrary"),
            vmem_limit_bytes=V7X_VMEM_LIMIT),
        name="in_proj",
    )(x, mod3, g1, w, gq, gk, cos_a, sin_a, cos_b, sin_b)


_MAX_FIXED_STABILISER = 50.0


def _attn_a_finish(acc, o_ref, tq):
    o = acc[:HEAD_DIM] * (1.0 / acc[HEAD_DIM:HEAD_DIM + 1])
    o_ref[0] = jnp.concatenate(
        [o[:, hq * tq:(hq + 1) * tq].T for hq in range(GROUP)], axis=1).astype(BF16)


def _bound_stats(qn, kn):
    b = qn.shape[0]
    kmax2 = jnp.max(kn, axis=(2, 3))
    qmax2 = jnp.max(qn.reshape(b, KV_HEADS, -1), axis=2)
    return jnp.stack([kmax2, qmax2 * kmax2], axis=-1).reshape(-1)


def _read_bound_stats(stats_ref):
    base = (pl.program_id(0) * KV_HEADS + pl.program_id(1)) * 2
    return stats_ref[base], stats_ref[base + 1] <= _MAX_FIXED_STABILISER ** 2


def _attn_a_kernel(stats_ref, q_ref, k_ref, v_ref, qn_ref, o_ref, *, tk, ahead, unroll):
    tq = q_ref.shape[3]
    n = GROUP * tq
    n_chunks = k_ref.shape[2] // tk
    q = jnp.concatenate([q_ref[0, hq] for hq in range(GROUP)], axis=1)
    kmax2, fixed_ok = _read_bound_stats(stats_ref)

    def chunk(c):
        start = pl.multiple_of(c * tk, tk)
        return k_ref[0, 0, pl.ds(start, tk), :], v_ref[0, 0, :, pl.ds(start, tk)]

    @pl.when(fixed_ok)
    def _():
        qn2 = jnp.concatenate([qn_ref[0, hq] for hq in range(GROUP)], axis=1)
        bound = jnp.sqrt(qn2 * kmax2)

        def scores(c):
            return jnp.dot(chunk(c)[0], q, preferred_element_type=F32)

        def consume(st, c, acc):
            p = jnp.exp2(st - bound).astype(BF16)
            return acc + jnp.dot(chunk(c)[1], p, preferred_element_type=F32)

        def body(c, carry):
            sts, acc = carry
            return sts[1:] + (scores(c + ahead),), consume(sts[0], c, acc)

        init = (tuple(scores(c) for c in range(ahead)), jnp.zeros((v_ref.shape[2], n), F32))
        sts, acc = lax.fori_loop(0, n_chunks - ahead, body, init, unroll=unroll)
        for j in range(ahead):
            acc = consume(sts[j], n_chunks - ahead + j, acc)
        _attn_a_finish(acc, o_ref, tq)

    @pl.when(jnp.logical_not(fixed_ok))
    def _():
        def body(c, carry):
            m, acc = carry
            kc, vc = chunk(c)
            st = jnp.dot(kc, q, preferred_element_type=F32)
            m_new = jnp.maximum(m, jnp.max(st, axis=0, keepdims=True))
            p = jnp.exp2(st - m_new).astype(BF16)
            acc = acc * jnp.exp2(m - m_new) + jnp.dot(vc, p, preferred_element_type=F32)
            return m_new, acc

        init = (jnp.full((1, n), NEG_INF, F32), jnp.zeros((v_ref.shape[2], n), F32))
        _, acc = lax.fori_loop(0, n_chunks, body, init)
        _attn_a_finish(acc, o_ref, tq)


def _attn_a_call(qt, k, vt, qn, kn, *, tq=ATTN_A_QUERIES, tk=ATTN_A_KEYS, ahead=1, unroll=True):
    b, _, _, s = qt.shape
    assert s % tq == 0 and s % tk == 0 and s // tk > ahead
    return pl.pallas_call(
        functools.partial(_attn_a_kernel, tk=tk, ahead=ahead, unroll=unroll),
        out_shape=jax.ShapeDtypeStruct((b, s, Q_HEADS * HEAD_DIM), BF16),
        grid=(b, KV_HEADS, s // tq),
        in_specs=[
            pl.BlockSpec(memory_space=pltpu.SMEM),
            pl.BlockSpec((1, GROUP, HEAD_DIM, tq), lambda bi, hk, i: (bi, hk, 0, i)),
            pl.BlockSpec((1, 1, s, HEAD_DIM), lambda bi, hk, i: (bi, hk, 0, 0)),
            pl.BlockSpec((1, 1, vt.shape[2], s), lambda bi, hk, i: (bi, hk, 0, 0)),
            pl.BlockSpec((1, GROUP, 1, tq), lambda bi, hk, i: (bi, hk, 0, i)),
        ],
        out_specs=pl.BlockSpec((1, tq, GROUP * HEAD_DIM), lambda bi, hk, i: (bi, i, hk)),
        compiler_params=pltpu.CompilerParams(
            dimension_semantics=("arbitrary", "arbitrary", "arbitrary"),
            vmem_limit_bytes=V7X_VMEM_LIMIT),
        name="attn_global",
    )(_bound_stats(qn, kn), qt, k, vt, qn)


def _attn_b_kernel(sink_ref, stats_ref, bias_ref, q_ref, kl_ref, kc_ref, kr_ref,
                   vl_ref, vc_ref, vr_ref, qn_ref, o_ref, *, nsub):
    hk = pl.program_id(1)
    i = pl.program_id(2)
    last_i = pl.num_programs(2) - 1
    w = kl_ref.shape[2]
    k = jnp.concatenate([kl_ref[0, 0], kc_ref[0, 0], kr_ref[0, 0]], axis=0)
    v = jnp.concatenate([vl_ref[0, 0], vc_ref[0, 0], vr_ref[0, 0]], axis=1)
    sink = jnp.concatenate(
        [jnp.full((1, w), sink_ref[hk * GROUP + hq] * LOG2E, F32) for hq in range(GROUP)],
        axis=1)
    band = bias_ref[1]
    kmax2, fixed_ok = _read_bound_stats(stats_ref)

    def block_lanes(ref, j):
        return jnp.concatenate([ref[0, hq, :, j * w:(j + 1) * w] for hq in range(GROUP)], axis=1)

    def masked_scores(j):
        bias = band
        if j == 0:
            bias = jnp.where(i == 0, bias_ref[0], band)
        if j == nsub - 1:
            bias = jnp.where(i == last_i, bias_ref[2], bias)
        st = jnp.dot(k[j * w:(j + 3) * w], block_lanes(q_ref, j),
                     preferred_element_type=F32)
        return st + jnp.concatenate([bias] * GROUP, axis=1)

    def finish(j, st, m):
        p = jnp.exp2(st - m).astype(BF16)
        acc = jnp.dot(v[:, j * w:(j + 3) * w], p, preferred_element_type=F32)
        denom = acc[HEAD_DIM:HEAD_DIM + 1] + jnp.exp2(sink - m)
        o = acc[:HEAD_DIM] * (1.0 / denom)
        o_ref[0, j * w:(j + 1) * w, :] = jnp.concatenate(
            [o[:, hq * w:(hq + 1) * w].T for hq in range(GROUP)], axis=1).astype(BF16)

    @pl.when(fixed_ok)
    def _():
        ahead = 2
        sts = [masked_scores(j) for j in range(min(ahead, nsub))]
        for j in range(nsub):
            if j + ahead < nsub:
                sts.append(masked_scores(j + ahead))
            bound = jnp.sqrt(block_lanes(qn_ref, j) * kmax2)
            finish(j, sts[j], jnp.maximum(bound, sink))

    @pl.when(jnp.logical_not(fixed_ok))
    def _():
        for j in range(nsub):
            st = masked_scores(j)
            finish(j, st, jnp.maximum(jnp.max(st, axis=0, keepdims=True), sink))


def _band_bias():
    w = WINDOW
    r = np.arange(3 * w)[:, None] - w
    c = np.arange(w)[None, :]
    band = np.abs(r - c) <= w
    first = band & (r >= 0)
    last = band & (r < w)
    table = np.stack([first, band, last]).astype(np.float32)
    return (1.0 - table) * NEG_INF


def _attn_b_call(sink, qt, k, vt, qn, kn, *, nsub=ATTN_B_BLOCKS):
    b, _, _, s = qt.shape
    assert s % (nsub * WINDOW) == 0
    w = WINDOW
    nb = s // w
    span = nsub * w
    bias = jnp.asarray(_band_bias(), F32)
    left = lambda i: jnp.maximum(i * nsub - 1, 0)
    right = lambda i: jnp.minimum((i + 1) * nsub, nb - 1)

    return pl.pallas_call(
        functools.partial(_attn_b_kernel, nsub=nsub),
        out_shape=jax.ShapeDtypeStruct((b, s, Q_HEADS * HEAD_DIM), BF16),
        grid=(b, KV_HEADS, s // span),
        in_specs=[
            pl.BlockSpec(memory_space=pltpu.SMEM),
            pl.BlockSpec(memory_space=pltpu.SMEM),
            pl.BlockSpec((3, 3 * w, w), lambda bi, hk, i: (0, 0, 0)),
            pl.BlockSpec((1, GROUP, HEAD_DIM, span), lambda bi, hk, i: (bi, hk, 0, i)),
            pl.BlockSpec((1, 1, w, HEAD_DIM), lambda bi, hk, i: (bi, hk, left(i), 0)),
            pl.BlockSpec((1, 1, span, HEAD_DIM), lambda bi, hk, i: (bi, hk, i, 0)),
            pl.BlockSpec((1, 1, w, HEAD_DIM), lambda bi, hk, i: (bi, hk, right(i), 0)),
            pl.BlockSpec((1, 1, vt.shape[2], w), lambda bi, hk, i: (bi, hk, 0, left(i))),
            pl.BlockSpec((1, 1, vt.shape[2], span), lambda bi, hk, i: (bi, hk, 0, i)),
            pl.BlockSpec((1, 1, vt.shape[2], w), lambda bi, hk, i: (bi, hk, 0, right(i))),
            pl.BlockSpec((1, GROUP, 1, span), lambda bi, hk, i: (bi, hk, 0, i)),
        ],
        out_specs=pl.BlockSpec((1, span, GROUP * HEAD_DIM), lambda bi, hk, i: (bi, i, hk)),
        compiler_params=pltpu.CompilerParams(
            dimension_semantics=("arbitrary", "arbitrary", "arbitrary"),
            vmem_limit_bytes=V7X_VMEM_LIMIT),
        name="attn_window",
    )(sink, _bound_stats(qn, kn), bias, qt, k, k, k, vt, vt, vt, qn)


def _channel_kernel(x_ref, mod_ref, ya_ref, yb_ref, sg_ref, wb_ref, wo_ref,
                    g2_ref, gf_ref, w1_ref, w2_ref, o_ref, *, n_sub, n_chunks):
    d = x_ref.shape[2]
    ts = x_ref.shape[1] // n_sub
    cw = w1_ref.shape[1] // n_chunks
    dot = functools.partial(jnp.dot, preferred_element_type=F32)
    gate1 = mod_ref[0, 2:3, :]
    shift2 = mod_ref[0, 3:4, :]
    scale2 = mod_ref[0, 4:5, :]
    gate2 = mod_ref[0, 5:6, :]
    for sub in range(n_sub):
        tok = slice(sub * ts, (sub + 1) * ts)
        ua = dot(ya_ref[0, tok, :], wb_ref[0])
        ub = dot(yb_ref[0, tok, :], wb_ref[1])
        sg = sg_ref[0, tok, :]
        merged = sg[:, :d].astype(F32) * ua + sg[:, d:].astype(F32) * ub
        x1 = x_ref[0, tok, :] + gate1 * dot(merged.astype(BF16), wo_ref[...])

        h2 = (_rms(x1) * g2_ref[...] * (1.0 + scale2) + shift2).astype(BF16)
        acc = jnp.zeros((ts, d), F32)
        for c in range(n_chunks):
            hid = dot(h2, w1_ref[:, c * cw:(c + 1) * cw])
            hid = jnp.square(jnp.maximum(hid, 0.0)).astype(BF16)
            acc = acc + dot(hid, w2_ref[c * cw:(c + 1) * cw, :])
        x2 = x1 + gate2 * acc
        o_ref[0, tok, :] = _rms(x2) * gf_ref[...]


def _channel_call(x, mod3, ya, yb, sg, wb, wo, g2, gf, w1, w2, *,
                  tm=CHANNEL_ROWS, n_sub=CHANNEL_SUBTILES, n_chunks=MLP_CHUNKS):
    b, s, d = x.shape
    assert s % tm == 0 and tm % n_sub == 0 and w1.shape[1] % n_chunks == 0
    bw = ya.shape[2]
    ff = w1.shape[1]
    tok = lambda bi, i: (bi, i, 0)
    const2 = lambda bi, i: (0, 0)
    once = pl.Buffered(1)
    return pl.pallas_call(
        functools.partial(_channel_kernel, n_sub=n_sub, n_chunks=n_chunks),
        out_shape=jax.ShapeDtypeStruct((b, s, d), F32),
        grid=(b, s // tm),
        in_specs=[
            pl.BlockSpec((1, tm, d), tok),
            pl.BlockSpec((1, 6, d), lambda bi, i: (bi, 0, 0)),
            pl.BlockSpec((1, tm, bw), tok),
            pl.BlockSpec((1, tm, bw), tok),
            pl.BlockSpec((1, tm, 2 * d), tok),
            pl.BlockSpec((2, bw, d), lambda bi, i: (0, 0, 0), pipeline_mode=once),
            pl.BlockSpec((d, d), const2, pipeline_mode=once),
            pl.BlockSpec((1, d), const2),
            pl.BlockSpec((1, d), const2),
            pl.BlockSpec((d, ff), const2, pipeline_mode=once),
            pl.BlockSpec((ff, d), const2, pipeline_mode=once),
        ],
        out_specs=pl.BlockSpec((1, tm, d), tok),
        compiler_params=pltpu.CompilerParams(
            dimension_semantics=("arbitrary", "arbitrary"),
            vmem_limit_bytes=V7X_VMEM_LIMIT),
        name="merge_mlp_final_norm",
    )(x, mod3, ya, yb, sg, wb, wo, g2, gf, w1, w2)


def _rope_tables(s):
    def cos_sin(pos, dim, theta):
        inv = theta ** (-np.arange(0, dim, 2, dtype=np.float64) / dim)
        ang = pos.astype(np.float64)[:, None] * inv[None, :]
        return np.cos(ang).T, np.sin(ang).T

    t = np.arange(s)
    cr, sr = cos_sin(t // GRID_W, HEAD_DIM // 2, AXIAL_THETA)
    cc, sc = cos_sin(t % GRID_W, HEAD_DIM // 2, AXIAL_THETA)
    c1, s1 = cos_sin(t, HEAD_DIM, ROPE_THETA)
    tables = (np.concatenate([cr, cr, cc, cc]), np.concatenate([-sr, sr, -sc, sc]),
              np.concatenate([c1, c1]), np.concatenate([-s1, s1]))
    return tuple(jnp.asarray(tab.astype(np.float32)) for tab in tables)


def kernel(x, c, w_ada, b_ada, norm1_g, w_in, q_norm_a, k_norm_a, sink_b, w_branch, w_out,
           norm2_g, w_mlp_in, w_mlp_out, final_g):
    b, s, d = x.shape
    assert w_ada.shape[0] == 1, "kernel supports DEPTH == 1"
    l = 0
    cos_a, sin_a, cos_b, sin_b = _rope_tables(s)

    mod3 = _mod_call(c, w_ada[l], b_ada[l][None, :]).reshape(b, 6, d)
    qa, ka, va, qna, kna, qb, kb, vb, qnb, knb, sg = _proj_call(
        x, mod3, norm1_g[l][None, :], w_in[l].astype(BF16),
        q_norm_a[l][:, None], k_norm_a[l][:, None], cos_a, sin_a, cos_b, sin_b)
    ya = _attn_a_call(qa, ka, va, qna, kna)
    yb = _attn_b_call(sink_b[l], qb, kb, vb, qnb, knb)
    return _channel_call(x, mod3, ya, yb, sg, w_branch[l].astype(BF16), w_out[l].astype(BF16),
                         norm2_g[l][None, :], final_g[None, :],
                         w_mlp_in[l].astype(BF16), w_mlp_out[l].astype(BF16))
```

```python
import functools
import math

import jax
import jax.numpy as jnp
import numpy as np
from jax import lax
from jax.experimental import pallas as pl
from jax.experimental.pallas import tpu as pltpu

F32 = jnp.float32
BF16 = jnp.bfloat16

HEAD_DIM = 64
Q_HEADS = 8
KV_HEADS = 2
GROUP = Q_HEADS // KV_HEADS
GRID_W = 64
WINDOW = 128
ROPE_THETA = 10000.0
AXIAL_THETA = 10000.0
NORM_EPS = 1e-6
NEG_INF = -1e30
LOG2E = math.log2(math.e)

V7X_BF16_SUBLANES = 16
V7X_VMEM_LIMIT = 56 * 1024 * 1024

MOD_COLS = 2048
PROJ_ROWS, PROJ_SUBTILES = 1024, 4
ATTN_A_QUERIES, ATTN_A_KEYS = 256, 256
ATTN_B_BLOCKS = 16
CHANNEL_ROWS, CHANNEL_SUBTILES, MLP_CHUNKS = 1024, 2, 4

V_ROWS_A = 80
V_ROWS_B = HEAD_DIM + V7X_BF16_SUBLANES

_BF16_SQUARE_MARGIN = (1.0 + 2.0 ** -8) ** 2


def _rms(x, eps=NORM_EPS):
    return x * lax.rsqrt(jnp.mean(x * x, axis=-1, keepdims=True) + eps)


def _mod_kernel(c_ref, w_ref, b_ref, o_ref):
    c = c_ref[...]
    a = c * (1.0 / (1.0 + jnp.exp(-c)))
    w = w_ref[...]
    a_hi = a.astype(BF16)
    a_lo = (a - a_hi.astype(F32)).astype(BF16)
    w_hi = w.astype(BF16)
    w_lo = (w - w_hi.astype(F32)).astype(BF16)
    dot = functools.partial(jnp.dot, preferred_element_type=F32)
    o_ref[...] = dot(a_hi, w_hi) + (dot(a_hi, w_lo) + dot(a_lo, w_hi)) + b_ref[...]


def _mod_call(c, w_ada, b_ada, *, tn=MOD_COLS):
    rows, d = c.shape
    n = w_ada.shape[1]
    return pl.pallas_call(
        _mod_kernel,
        out_shape=jax.ShapeDtypeStruct((rows, n), F32),
        grid=(n // tn,),
        in_specs=[
            pl.BlockSpec((rows, d), lambda j: (0, 0)),
            pl.BlockSpec((d, tn), lambda j: (0, j)),
            pl.BlockSpec((1, tn), lambda j: (0, j)),
        ],
        out_specs=pl.BlockSpec((rows, tn), lambda j: (0, j)),
        compiler_params=pltpu.CompilerParams(
            dimension_semantics=("arbitrary",), vmem_limit_bytes=V7X_VMEM_LIMIT),
        name="adaln_mod",
    )(c, w_ada, b_ada)


def _rope_t(x, cos, sin, half):
    parts = []
    for base in range(0, HEAD_DIM, 2 * half):
        parts.append(x[base + half:base + 2 * half])
        parts.append(x[base:base + half])
    x_rot = jnp.concatenate(parts, axis=0)
    return x * cos + x_rot * sin


def _proj_kernel(x_ref, mod_ref, g1_ref, w_ref, gq_ref, gk_ref,
                 cos_a_ref, sin_a_ref, cos_b_ref, sin_b_ref,
                 qa_ref, ka_ref, va_ref, qna_ref, kna_ref,
                 qb_ref, kb_ref, vb_ref, qnb_ref, knb_ref, sg_ref, wt_ref, *, n_sub, n_qkv):
    @pl.when((pl.program_id(0) == 0) & (pl.program_id(1) == 0))
    def _():
        wt_ref[...] = w_ref[:, :n_qkv].T

    shift1 = mod_ref[0, 0:1, :]
    scale1 = mod_ref[0, 1:2, :]
    g1 = g1_ref[...]
    gq, gk = gq_ref[...], gk_ref[...]
    q_scale = HEAD_DIM ** -0.5 * LOG2E
    ts = x_ref.shape[1] // n_sub
    ones_a = jnp.ones((va_ref.shape[2] - HEAD_DIM, ts), BF16)
    ones_b = jnp.ones((vb_ref.shape[2] - HEAD_DIM, ts), BF16)

    def normed(t, g):
        ms = jnp.sum(t * t, axis=0, keepdims=True) * (1.0 / HEAD_DIM)
        return t * lax.rsqrt(ms + NORM_EPS) * g

    def norm2_bound(t):
        return jnp.sum(t * t, axis=0, keepdims=True) * _BF16_SQUARE_MARGIN

    for sub in range(n_sub):
        tok = slice(sub * ts, (sub + 1) * ts)
        x = x_ref[0, tok, :]
        h = (_rms(x) * g1 * (1.0 + scale1) + shift1).astype(BF16)

        pt = lax.dot_general(wt_ref[...], h, (((1,), (1,)), ((), ())),
                             preferred_element_type=F32)
        cos_a, sin_a = cos_a_ref[:, tok], sin_a_ref[:, tok]
        cos_b, sin_b = cos_b_ref[:, tok], sin_b_ref[:, tok]

        def head(row):
            return pt[row:row + HEAD_DIM]

        row = 0
        for hq in range(Q_HEADS):
            t = _rope_t(normed(head(row), gq), cos_a, sin_a, HEAD_DIM // 4) * q_scale
            qa_ref[0, hq, :, tok] = t.astype(BF16)
            qna_ref[0, hq, :, tok] = norm2_bound(t)
            row += HEAD_DIM
        for hk in range(KV_HEADS):
            t = _rope_t(normed(head(row), gk), cos_a, sin_a, HEAD_DIM // 4)
            ka_ref[0, hk, tok, :] = t.T.astype(BF16)
            kna_ref[0, hk, :, tok] = norm2_bound(t)
            row += HEAD_DIM
        for hk in range(KV_HEADS):
            va_ref[0, hk, :, tok] = jnp.concatenate([head(row).astype(BF16), ones_a], axis=0)
            row += HEAD_DIM
        for hq in range(Q_HEADS):
            t = _rope_t(head(row), cos_b, sin_b, HEAD_DIM // 2) * q_scale
            qb_ref[0, hq, :, tok] = t.astype(BF16)
            qnb_ref[0, hq, :, tok] = norm2_bound(t)
            row += HEAD_DIM
        for hk in range(KV_HEADS):
            t = _rope_t(head(row), cos_b, sin_b, HEAD_DIM // 2)
            kb_ref[0, hk, tok, :] = t.T.astype(BF16)
            knb_ref[0, hk, :, tok] = norm2_bound(t)
            row += HEAD_DIM
        for hk in range(KV_HEADS):
            vb_ref[0, hk, :, tok] = jnp.concatenate([head(row).astype(BF16), ones_b], axis=0)
            row += HEAD_DIM

        g = jnp.dot(h, w_ref[:, n_qkv:], preferred_element_type=F32)
        sg_ref[0, tok, :] = (0.5 * jnp.tanh(0.5 * g) + 0.5).astype(BF16)


def _proj_call(x, mod3, g1, w, gq, gk, cos_a, sin_a, cos_b, sin_b, *,
               tm=PROJ_ROWS, n_sub=PROJ_SUBTILES):
    b, s, d = x.shape
    assert s % tm == 0 and tm % n_sub == 0
    n_qkv = 2 * (Q_HEADS + 2 * KV_HEADS) * HEAD_DIM
    n_g = w.shape[1] - n_qkv
    const2 = lambda bi, i: (0, 0)
    tab = pl.BlockSpec((HEAD_DIM, tm), lambda bi, i: (0, i))
    q_shape = jax.ShapeDtypeStruct((b, Q_HEADS, HEAD_DIM, s), BF16)
    k_shape = jax.ShapeDtypeStruct((b, KV_HEADS, s, HEAD_DIM), BF16)
    q_spec = pl.BlockSpec((1, Q_HEADS, HEAD_DIM, tm), lambda bi, i: (bi, 0, 0, i))
    k_spec = pl.BlockSpec((1, KV_HEADS, tm, HEAD_DIM), lambda bi, i: (bi, 0, i, 0))

    def v_out(rows):
        return (jax.ShapeDtypeStruct((b, KV_HEADS, rows, s), BF16),
                pl.BlockSpec((1, KV_HEADS, rows, tm), lambda bi, i: (bi, 0, 0, i)))

    (va_shape, va_spec), (vb_shape, vb_spec) = v_out(V_ROWS_A), v_out(V_ROWS_B)
    qn_shape = jax.ShapeDtypeStruct((b, Q_HEADS, 1, s), F32)
    qn_spec = pl.BlockSpec((1, Q_HEADS, 1, tm), lambda bi, i: (bi, 0, 0, i))
    kn_shape = jax.ShapeDtypeStruct((b, KV_HEADS, 1, s), F32)
    kn_spec = pl.BlockSpec((1, KV_HEADS, 1, tm), lambda bi, i: (bi, 0, 0, i))
    a_shapes = (q_shape, k_shape, va_shape, qn_shape, kn_shape)
    a_specs = (q_spec, k_spec, va_spec, qn_spec, kn_spec)
    b_shapes = (q_shape, k_shape, vb_shape, qn_shape, kn_shape)
    b_specs = (q_spec, k_spec, vb_spec, qn_spec, kn_spec)
    return pl.pallas_call(
        functools.partial(_proj_kernel, n_sub=n_sub, n_qkv=n_qkv),
        out_shape=a_shapes + b_shapes + (jax.ShapeDtypeStruct((b, s, n_g), BF16),),
        grid=(b, s // tm),
        in_specs=[
            pl.BlockSpec((1, tm, d), lambda bi, i: (bi, i, 0)),
            pl.BlockSpec((1, 6, d), lambda bi, i: (bi, 0, 0)),
            pl.BlockSpec((1, d), const2),
            pl.BlockSpec((d, n_qkv + n_g), const2, pipeline_mode=pl.Buffered(1)),
            pl.BlockSpec((HEAD_DIM, 1), const2),
            pl.BlockSpec((HEAD_DIM, 1), const2),
            tab, tab, tab, tab,
        ],
        out_specs=a_specs + b_specs + (
            pl.BlockSpec((1, tm, n_g), lambda bi, i: (bi, i, 0)),),
        scratch_shapes=[pltpu.VMEM((n_qkv, d), BF16)],
        compiler_params=pltpu.CompilerParams(
            dimension_semantics=("arbitrary", "arbitrary"),
            vmem_limit_bytes=V7X_VMEM_LIMIT),
        name="in_proj",
    )(x, mod3, g1, w, gq, gk, cos_a, sin_a, cos_b, sin_b)


_MAX_FIXED_STABILISER = 50.0


def _attn_a_finish(acc, o_ref, tq):
    o = acc[:HEAD_DIM] * (1.0 / acc[HEAD_DIM:HEAD_DIM + 1])
    o_ref[0] = jnp.concatenate(
        [o[:, hq * tq:(hq + 1) * tq].T for hq in range(GROUP)], axis=1).astype(BF16)


def _bound_stats(qn, kn):
    b = qn.shape[0]
    kmax2 = jnp.max(kn, axis=(2, 3))
    qmax2 = jnp.max(qn.reshape(b, KV_HEADS, -1), axis=2)
    return jnp.stack([kmax2, qmax2 * kmax2], axis=-1).reshape(-1)


def _read_bound_stats(stats_ref):
    base = (pl.program_id(0) * KV_HEADS + pl.program_id(1)) * 2
    return stats_ref[base], stats_ref[base + 1] <= _MAX_FIXED_STABILISER ** 2


def _attn_a_kernel(stats_ref, q_ref, k_ref, v_ref, qn_ref, o_ref, *, tk, ahead, unroll):
    tq = q_ref.shape[3]
    n = GROUP * tq
    n_chunks = k_ref.shape[2] // tk
    q = jnp.concatenate([q_ref[0, hq] for hq in range(GROUP)], axis=1)
    kmax2, fixed_ok = _read_bound_stats(stats_ref)

    def chunk(c):
        start = pl.multiple_of(c * tk, tk)
        return k_ref[0, 0, pl.ds(start, tk), :], v_ref[0, 0, :, pl.ds(start, tk)]

    @pl.when(fixed_ok)
    def _():
        qn2 = jnp.concatenate([qn_ref[0, hq] for hq in range(GROUP)], axis=1)
        bound = jnp.sqrt(qn2 * kmax2)

        lanes = [slice(t * 256, (t + 1) * 256) for t in range(n // 256)]

        def scores(c, sl):
            return jnp.dot(chunk(c)[0], q[:, sl], preferred_element_type=F32)

        def consume(st, c, sl, acc):
            p = jnp.exp2(st - bound[:, sl]).astype(BF16)
            return acc + jnp.dot(chunk(c)[1], p, preferred_element_type=F32)

        def body(c, carry):
            sts, accs = carry
            new_sts, new_accs = [], []
            for t, sl in enumerate(lanes):
                new_sts.append(scores(c + 1, sl))
                new_accs.append(consume(sts[t], c, sl, accs[t]))
            return tuple(new_sts), tuple(new_accs)

        init = (tuple(scores(0, sl) for sl in lanes),
                tuple(jnp.zeros((v_ref.shape[2], 256), F32) for _ in lanes))
        sts, accs = lax.fori_loop(0, n_chunks - 1, body, init, unroll=unroll)
        acc = jnp.concatenate(
            [consume(sts[t], n_chunks - 1, sl, accs[t]) for t, sl in enumerate(lanes)], axis=1)
        _attn_a_finish(acc, o_ref, tq)

    @pl.when(jnp.logical_not(fixed_ok))
    def _():
        def body(c, carry):
            m, acc = carry
            kc, vc = chunk(c)
            st = jnp.dot(kc, q, preferred_element_type=F32)
            m_new = jnp.maximum(m, jnp.max(st, axis=0, keepdims=True))
            p = jnp.exp2(st - m_new).astype(BF16)
            acc = acc * jnp.exp2(m - m_new) + jnp.dot(vc, p, preferred_element_type=F32)
            return m_new, acc

        init = (jnp.full((1, n), NEG_INF, F32), jnp.zeros((v_ref.shape[2], n), F32))
        _, acc = lax.fori_loop(0, n_chunks, body, init)
        _attn_a_finish(acc, o_ref, tq)


def _attn_a_call(qt, k, vt, qn, kn, *, tq=ATTN_A_QUERIES, tk=ATTN_A_KEYS, ahead=1, unroll=True):
    b, _, _, s = qt.shape
    assert s % tq == 0 and s % tk == 0 and s // tk > ahead
    return pl.pallas_call(
        functools.partial(_attn_a_kernel, tk=tk, ahead=ahead, unroll=unroll),
        out_shape=jax.ShapeDtypeStruct((b, s, Q_HEADS * HEAD_DIM), BF16),
        grid=(b, KV_HEADS, s // tq),
        in_specs=[
            pl.BlockSpec(memory_space=pltpu.SMEM),
            pl.BlockSpec((1, GROUP, HEAD_DIM, tq), lambda bi, hk, i: (bi, hk, 0, i)),
            pl.BlockSpec((1, 1, s, HEAD_DIM), lambda bi, hk, i: (bi, hk, 0, 0)),
            pl.BlockSpec((1, 1, vt.shape[2], s), lambda bi, hk, i: (bi, hk, 0, 0)),
            pl.BlockSpec((1, GROUP, 1, tq), lambda bi, hk, i: (bi, hk, 0, i)),
        ],
        out_specs=pl.BlockSpec((1, tq, GROUP * HEAD_DIM), lambda bi, hk, i: (bi, i, hk)),
        compiler_params=pltpu.CompilerParams(
            dimension_semantics=("arbitrary", "arbitrary", "arbitrary"),
            vmem_limit_bytes=V7X_VMEM_LIMIT),
        name="attn_global",
    )(_bound_stats(qn, kn), qt, k, vt, qn)


def _attn_b_kernel(sink_ref, stats_ref, bias_ref, q_ref, kl_ref, kc_ref, kr_ref,
                   vl_ref, vc_ref, vr_ref, qn_ref, o_ref, *, nsub):
    hk = pl.program_id(1)
    i = pl.program_id(2)
    last_i = pl.num_programs(2) - 1
    w = kl_ref.shape[2]
    k = jnp.concatenate([kl_ref[0, 0], kc_ref[0, 0], kr_ref[0, 0]], axis=0)
    v = jnp.concatenate([vl_ref[0, 0], vc_ref[0, 0], vr_ref[0, 0]], axis=1)
    sink = jnp.concatenate(
        [jnp.full((1, w), sink_ref[hk * GROUP + hq] * LOG2E, F32) for hq in range(GROUP)],
        axis=1)
    band = bias_ref[1]
    kmax2, fixed_ok = _read_bound_stats(stats_ref)

    def block_lanes(ref, j):
        return jnp.concatenate([ref[0, hq, :, j * w:(j + 1) * w] for hq in range(GROUP)], axis=1)

    def masked_scores(j):
        bias = band
        if j == 0:
            bias = jnp.where(i == 0, bias_ref[0], band)
        if j == nsub - 1:
            bias = jnp.where(i == last_i, bias_ref[2], bias)
        st = jnp.dot(k[j * w:(j + 3) * w], block_lanes(q_ref, j),
                     preferred_element_type=F32)
        return st + jnp.concatenate([bias] * GROUP, axis=1)

    def finish(j, st, m):
        p = jnp.exp2(st - m).astype(BF16)
        acc = jnp.dot(v[:, j * w:(j + 3) * w], p, preferred_element_type=F32)
        denom = acc[HEAD_DIM:HEAD_DIM + 1] + jnp.exp2(sink - m)
        o = acc[:HEAD_DIM] * (1.0 / denom)
        o_ref[0, j * w:(j + 1) * w, :] = jnp.concatenate(
            [o[:, hq * w:(hq + 1) * w].T for hq in range(GROUP)], axis=1).astype(BF16)

    @pl.when(fixed_ok)
    def _():
        ahead = 2
        sts = [masked_scores(j) for j in range(min(ahead, nsub))]
        for j in range(nsub):
            if j + ahead < nsub:
                sts.append(masked_scores(j + ahead))
            bound = jnp.sqrt(block_lanes(qn_ref, j) * kmax2)
            finish(j, sts[j], jnp.maximum(bound, sink))

    @pl.when(jnp.logical_not(fixed_ok))
    def _():
        for j in range(nsub):
            st = masked_scores(j)
            finish(j, st, jnp.maximum(jnp.max(st, axis=0, keepdims=True), sink))


def _band_bias():
    w = WINDOW
    r = np.arange(3 * w)[:, None] - w
    c = np.arange(w)[None, :]
    band = np.abs(r - c) <= w
    first = band & (r >= 0)
    last = band & (r < w)
    table = np.stack([first, band, last]).astype(np.float32)
    return (1.0 - table) * NEG_INF


def _attn_b_call(sink, qt, k, vt, qn, kn, *, nsub=ATTN_B_BLOCKS):
    b, _, _, s = qt.shape
    assert s % (nsub * WINDOW) == 0
    w = WINDOW
    nb = s // w
    span = nsub * w
    bias = jnp.asarray(_band_bias(), F32)
    left = lambda i: jnp.maximum(i * nsub - 1, 0)
    right = lambda i: jnp.minimum((i + 1) * nsub, nb - 1)

    return pl.pallas_call(
        functools.partial(_attn_b_kernel, nsub=nsub),
        out_shape=jax.ShapeDtypeStruct((b, s, Q_HEADS * HEAD_DIM), BF16),
        grid=(b, KV_HEADS, s // span),
        in_specs=[
            pl.BlockSpec(memory_space=pltpu.SMEM),
            pl.BlockSpec(memory_space=pltpu.SMEM),
            pl.BlockSpec((3, 3 * w, w), lambda bi, hk, i: (0, 0, 0)),
            pl.BlockSpec((1, GROUP, HEAD_DIM, span), lambda bi, hk, i: (bi, hk, 0, i)),
            pl.BlockSpec((1, 1, w, HEAD_DIM), lambda bi, hk, i: (bi, hk, left(i), 0)),
            pl.BlockSpec((1, 1, span, HEAD_DIM), lambda bi, hk, i: (bi, hk, i, 0)),
            pl.BlockSpec((1, 1, w, HEAD_DIM), lambda bi, hk, i: (bi, hk, right(i), 0)),
            pl.BlockSpec((1, 1, vt.shape[2], w), lambda bi, hk, i: (bi, hk, 0, left(i))),
            pl.BlockSpec((1, 1, vt.shape[2], span), lambda bi, hk, i: (bi, hk, 0, i)),
            pl.BlockSpec((1, 1, vt.shape[2], w), lambda bi, hk, i: (bi, hk, 0, right(i))),
            pl.BlockSpec((1, GROUP, 1, span), lambda bi, hk, i: (bi, hk, 0, i)),
        ],
        out_specs=pl.BlockSpec((1, span, GROUP * HEAD_DIM), lambda bi, hk, i: (bi, i, hk)),
        compiler_params=pltpu.CompilerParams(
            dimension_semantics=("arbitrary", "arbitrary", "arbitrary"),
            vmem_limit_bytes=V7X_VMEM_LIMIT),
        name="attn_window",
    )(sink, _bound_stats(qn, kn), bias, qt, k, k, k, vt, vt, vt, qn)


def _channel_kernel(x_ref, mod_ref, ya_ref, yb_ref, sg_ref, wb_ref, wo_ref,
                    g2_ref, gf_ref, w1_ref, w2_ref, o_ref, *, n_sub, n_chunks):
    d = x_ref.shape[2]
    ts = x_ref.shape[1] // n_sub
    cw = w1_ref.shape[1] // n_chunks
    dot = functools.partial(jnp.dot, preferred_element_type=F32)
    gate1 = mod_ref[0, 2:3, :]
    shift2 = mod_ref[0, 3:4, :]
    scale2 = mod_ref[0, 4:5, :]
    gate2 = mod_ref[0, 5:6, :]
    for sub in range(n_sub):
        tok = slice(sub * ts, (sub + 1) * ts)
        ua = dot(ya_ref[0, tok, :], wb_ref[0])
        ub = dot(yb_ref[0, tok, :], wb_ref[1])
        sg = sg_ref[0, tok, :]
        merged = sg[:, :d].astype(F32) * ua + sg[:, d:].astype(F32) * ub
        x1 = x_ref[0, tok, :] + gate1 * dot(merged.astype(BF16), wo_ref[...])

        h2 = (_rms(x1) * g2_ref[...] * (1.0 + scale2) + shift2).astype(BF16)
        acc = jnp.zeros((ts, d), F32)
        for c in range(n_chunks):
            hid = dot(h2, w1_ref[:, c * cw:(c + 1) * cw])
            hid = jnp.square(jnp.maximum(hid, 0.0)).astype(BF16)
            acc = acc + dot(hid, w2_ref[c * cw:(c + 1) * cw, :])
        x2 = x1 + gate2 * acc
        o_ref[0, tok, :] = _rms(x2) * gf_ref[...]


def _channel_call(x, mod3, ya, yb, sg, wb, wo, g2, gf, w1, w2, *,
                  tm=CHANNEL_ROWS, n_sub=CHANNEL_SUBTILES, n_chunks=MLP_CHUNKS):
    b, s, d = x.shape
    assert s % tm == 0 and tm % n_sub == 0 and w1.shape[1] % n_chunks == 0
    bw = ya.shape[2]
    ff = w1.shape[1]
    tok = lambda bi, i: (bi, i, 0)
    const2 = lambda bi, i: (0, 0)
    once = pl.Buffered(1)
    return pl.pallas_call(
        functools.partial(_channel_kernel, n_sub=n_sub, n_chunks=n_chunks),
        out_shape=jax.ShapeDtypeStruct((b, s, d), F32),
        grid=(b, s // tm),
        in_specs=[
            pl.BlockSpec((1, tm, d), tok),
            pl.BlockSpec((1, 6, d), lambda bi, i: (bi, 0, 0)),
            pl.BlockSpec((1, tm, bw), tok),
            pl.BlockSpec((1, tm, bw), tok),
            pl.BlockSpec((1, tm, 2 * d), tok),
            pl.BlockSpec((2, bw, d), lambda bi, i: (0, 0, 0), pipeline_mode=once),
            pl.BlockSpec((d, d), const2, pipeline_mode=once),
            pl.BlockSpec((1, d), const2),
            pl.BlockSpec((1, d), const2),
            pl.BlockSpec((d, ff), const2, pipeline_mode=once),
            pl.BlockSpec((ff, d), const2, pipeline_mode=once),
        ],
        out_specs=pl.BlockSpec((1, tm, d), tok),
        compiler_params=pltpu.CompilerParams(
            dimension_semantics=("arbitrary", "arbitrary"),
            vmem_limit_bytes=V7X_VMEM_LIMIT),
        name="merge_mlp_final_norm",
    )(x, mod3, ya, yb, sg, wb, wo, g2, gf, w1, w2)


def _rope_tables(s):
    def cos_sin(pos, dim, theta):
        inv = theta ** (-np.arange(0, dim, 2, dtype=np.float64) / dim)
        ang = pos.astype(np.float64)[:, None] * inv[None, :]
        return np.cos(ang).T, np.sin(ang).T

    t = np.arange(s)
    cr, sr = cos_sin(t // GRID_W, HEAD_DIM // 2, AXIAL_THETA)
    cc, sc = cos_sin(t % GRID_W, HEAD_DIM // 2, AXIAL_THETA)
    c1, s1 = cos_sin(t, HEAD_DIM, ROPE_THETA)
    tables = (np.concatenate([cr, cr, cc, cc]), np.concatenate([-sr, sr, -sc, sc]),
              np.concatenate([c1, c1]), np.concatenate([-s1, s1]))
    return tuple(jnp.asarray(tab.astype(np.float32)) for tab in tables)


def kernel(x, c, w_ada, b_ada, norm1_g, w_in, q_norm_a, k_norm_a, sink_b, w_branch, w_out,
           norm2_g, w_mlp_in, w_mlp_out, final_g):
    b, s, d = x.shape
    assert w_ada.shape[0] == 1, "kernel supports DEPTH == 1"
    l = 0
    cos_a, sin_a, cos_b, sin_b = _rope_tables(s)

    mod3 = _mod_call(c, w_ada[l], b_ada[l][None, :]).reshape(b, 6, d)
    qa, ka, va, qna, kna, qb, kb, vb, qnb, knb, sg = _proj_call(
        x, mod3, norm1_g[l][None, :], w_in[l].astype(BF16),
        q_norm_a[l][:, None], k_norm_a[l][:, None], cos_a, sin_a, cos_b, sin_b)
    ya = _attn_a_call(qa, ka, va, qna, kna)
    yb = _attn_b_call(sink_b[l], qb, kb, vb, qnb, knb)
    return _channel_call(x, mod3, ya, yb, sg, w_branch[l].astype(BF16), w_out[l].astype(BF16),
                         norm2_g[l][None, :], final_g[None, :],
                         w_mlp_in[l].astype(BF16), w_mlp_out[l].astype(BF16))
```

```python
import functools
import math

import jax
import jax.numpy as jnp
import numpy as np
from jax import lax
from jax.experimental import pallas as pl
from jax.experimental.pallas import tpu as pltpu

F32 = jnp.float32
BF16 = jnp.bfloat16

HEAD_DIM = 64
Q_HEADS = 8
KV_HEADS = 2
GROUP = Q_HEADS // KV_HEADS
GRID_W = 64
WINDOW = 128
ROPE_THETA = 10000.0
AXIAL_THETA = 10000.0
NORM_EPS = 1e-6
NEG_INF = -1e30
LOG2E = math.log2(math.e)

V7X_BF16_SUBLANES = 16
V7X_MXU_COLS = 256
V7X_VMEM_LIMIT = 56 * 1024 * 1024

MOD_COLS = 2048
PROJ_ROWS, PROJ_SUBTILES = 1024, 4
ATTN_A_QUERIES, ATTN_A_KEYS = 512, 256
ATTN_B_BLOCKS = 16
CHANNEL_ROWS, CHANNEL_SUBTILES, MLP_CHUNKS = 1024, 2, 4

V_ROWS = HEAD_DIM + V7X_BF16_SUBLANES

_BF16_SQUARE_MARGIN = (1.0 + 2.0 ** -8) ** 2


def _rms(x, eps=NORM_EPS):
    return x * lax.rsqrt(jnp.mean(x * x, axis=-1, keepdims=True) + eps)


def _mod_kernel(c_ref, w_ref, b_ref, o_ref):
    c = c_ref[...]
    a = c * (1.0 / (1.0 + jnp.exp(-c)))
    w = w_ref[...]
    a_hi = a.astype(BF16)
    a_lo = (a - a_hi.astype(F32)).astype(BF16)
    w_hi = w.astype(BF16)
    w_lo = (w - w_hi.astype(F32)).astype(BF16)
    dot = functools.partial(jnp.dot, preferred_element_type=F32)
    o_ref[...] = dot(a_hi, w_hi) + (dot(a_hi, w_lo) + dot(a_lo, w_hi)) + b_ref[...]


def _mod_call(c, w_ada, b_ada, *, tn=MOD_COLS):
    rows, d = c.shape
    n = w_ada.shape[1]
    return pl.pallas_call(
        _mod_kernel,
        out_shape=jax.ShapeDtypeStruct((rows, n), F32),
        grid=(n // tn,),
        in_specs=[
            pl.BlockSpec((rows, d), lambda j: (0, 0)),
            pl.BlockSpec((d, tn), lambda j: (0, j)),
            pl.BlockSpec((1, tn), lambda j: (0, j)),
        ],
        out_specs=pl.BlockSpec((rows, tn), lambda j: (0, j)),
        compiler_params=pltpu.CompilerParams(
            dimension_semantics=("arbitrary",), vmem_limit_bytes=V7X_VMEM_LIMIT),
        name="adaln_mod",
    )(c, w_ada, b_ada)


def _rope_t(x, cos, sin, half):
    parts = []
    for base in range(0, HEAD_DIM, 2 * half):
        parts.append(x[base + half:base + 2 * half])
        parts.append(x[base:base + half])
    x_rot = jnp.concatenate(parts, axis=0)
    return x * cos + x_rot * sin


def _proj_kernel(x_ref, mod_ref, g1_ref, w_ref, gq_ref, gk_ref,
                 cos_a_ref, sin_a_ref, cos_b_ref, sin_b_ref,
                 qa_ref, ka_ref, va_ref, qna_ref, kna_ref,
                 qb_ref, kb_ref, vb_ref, qnb_ref, knb_ref, wt_ref, *, n_sub, n_qkv):
    @pl.when((pl.program_id(0) == 0) & (pl.program_id(1) == 0))
    def _():
        wt_ref[...] = w_ref[:, :n_qkv].T

    shift1 = mod_ref[0, 0:1, :]
    scale1 = mod_ref[0, 1:2, :]
    g1 = g1_ref[...]
    gq, gk = gq_ref[...], gk_ref[...]
    q_scale = HEAD_DIM ** -0.5 * LOG2E
    ts = x_ref.shape[1] // n_sub
    ones_a = jnp.ones((va_ref.shape[2] - HEAD_DIM, ts), BF16)
    ones_b = jnp.ones((vb_ref.shape[2] - HEAD_DIM, ts), BF16)

    def normed(t, g):
        ms = jnp.sum(t * t, axis=0, keepdims=True) * (1.0 / HEAD_DIM)
        return t * lax.rsqrt(ms + NORM_EPS) * g

    def norm2_bound(t):
        return jnp.sum(t * t, axis=0, keepdims=True) * _BF16_SQUARE_MARGIN

    for sub in range(n_sub):
        tok = slice(sub * ts, (sub + 1) * ts)
        x = x_ref[0, tok, :]
        h = (_rms(x) * g1 * (1.0 + scale1) + shift1).astype(BF16)

        pt = lax.dot_general(wt_ref[...], h, (((1,), (1,)), ((), ())),
                             preferred_element_type=F32)
        cos_a, sin_a = cos_a_ref[:, tok], sin_a_ref[:, tok]
        cos_b, sin_b = cos_b_ref[:, tok], sin_b_ref[:, tok]

        def head(row):
            return pt[row:row + HEAD_DIM]

        row = 0
        for hq in range(Q_HEADS):
            t = _rope_t(normed(head(row), gq), cos_a, sin_a, HEAD_DIM // 4) * q_scale
            qa_ref[0, hq, :, tok] = t.astype(BF16)
            qna_ref[0, hq, :, tok] = norm2_bound(t)
            row += HEAD_DIM
        for hk in range(KV_HEADS):
            t = _rope_t(normed(head(row), gk), cos_a, sin_a, HEAD_DIM // 4)
            ka_ref[0, hk, tok, :] = t.T.astype(BF16)
            kna_ref[0, hk, :, tok] = norm2_bound(t)
            row += HEAD_DIM
        for hk in range(KV_HEADS):
            va_ref[0, hk, :, tok] = jnp.concatenate([head(row).astype(BF16), ones_a], axis=0)
            row += HEAD_DIM
        for hq in range(Q_HEADS):
            t = _rope_t(head(row), cos_b, sin_b, HEAD_DIM // 2) * q_scale
            qb_ref[0, hq, :, tok] = t.astype(BF16)
            qnb_ref[0, hq, :, tok] = norm2_bound(t)
            row += HEAD_DIM
        for hk in range(KV_HEADS):
            t = _rope_t(head(row), cos_b, sin_b, HEAD_DIM // 2)
            kb_ref[0, hk, tok, :] = t.T.astype(BF16)
            knb_ref[0, hk, :, tok] = norm2_bound(t)
            row += HEAD_DIM
        for hk in range(KV_HEADS):
            vb_ref[0, hk, :, tok] = jnp.concatenate([head(row).astype(BF16), ones_b], axis=0)
            row += HEAD_DIM


def _proj_call(x, mod3, g1, w, gq, gk, cos_a, sin_a, cos_b, sin_b, *,
               tm=PROJ_ROWS, n_sub=PROJ_SUBTILES):
    b, s, d = x.shape
    assert s % tm == 0 and tm % n_sub == 0
    n_qkv = 2 * (Q_HEADS + 2 * KV_HEADS) * HEAD_DIM
    const2 = lambda bi, i: (0, 0)
    tab = pl.BlockSpec((HEAD_DIM, tm), lambda bi, i: (0, i))
    q_shape = jax.ShapeDtypeStruct((b, Q_HEADS, HEAD_DIM, s), BF16)
    k_shape = jax.ShapeDtypeStruct((b, KV_HEADS, s, HEAD_DIM), BF16)
    q_spec = pl.BlockSpec((1, Q_HEADS, HEAD_DIM, tm), lambda bi, i: (bi, 0, 0, i))
    k_spec = pl.BlockSpec((1, KV_HEADS, tm, HEAD_DIM), lambda bi, i: (bi, 0, i, 0))

    def v_out(rows):
        return (jax.ShapeDtypeStruct((b, KV_HEADS, rows, s), BF16),
                pl.BlockSpec((1, KV_HEADS, rows, tm), lambda bi, i: (bi, 0, 0, i)))

    (va_shape, va_spec), (vb_shape, vb_spec) = v_out(V_ROWS), v_out(V_ROWS)
    qn_shape = jax.ShapeDtypeStruct((b, Q_HEADS, 1, s), F32)
    qn_spec = pl.BlockSpec((1, Q_HEADS, 1, tm), lambda bi, i: (bi, 0, 0, i))
    kn_shape = jax.ShapeDtypeStruct((b, KV_HEADS, 1, s), F32)
    kn_spec = pl.BlockSpec((1, KV_HEADS, 1, tm), lambda bi, i: (bi, 0, 0, i))
    a_shapes = (q_shape, k_shape, va_shape, qn_shape, kn_shape)
    a_specs = (q_spec, k_spec, va_spec, qn_spec, kn_spec)
    b_shapes = (q_shape, k_shape, vb_shape, qn_shape, kn_shape)
    b_specs = (q_spec, k_spec, vb_spec, qn_spec, kn_spec)
    return pl.pallas_call(
        functools.partial(_proj_kernel, n_sub=n_sub, n_qkv=n_qkv),
        out_shape=a_shapes + b_shapes,
        grid=(b, s // tm),
        in_specs=[
            pl.BlockSpec((1, tm, d), lambda bi, i: (bi, i, 0)),
            pl.BlockSpec((1, 6, d), lambda bi, i: (bi, 0, 0)),
            pl.BlockSpec((1, d), const2),
            pl.BlockSpec((d, n_qkv), const2, pipeline_mode=pl.Buffered(1)),
            pl.BlockSpec((HEAD_DIM, 1), const2),
            pl.BlockSpec((HEAD_DIM, 1), const2),
            tab, tab, tab, tab,
        ],
        out_specs=a_specs + b_specs,
        scratch_shapes=[pltpu.VMEM((n_qkv, d), BF16)],
        compiler_params=pltpu.CompilerParams(
            dimension_semantics=("arbitrary", "arbitrary"),
            vmem_limit_bytes=V7X_VMEM_LIMIT),
        name="in_proj",
    )(x, mod3, g1, w, gq, gk, cos_a, sin_a, cos_b, sin_b)


_MAX_FIXED_STABILISER = 50.0


def _attn_a_finish(acc, o_ref, tq):
    o = acc[:HEAD_DIM] * (1.0 / acc[HEAD_DIM:HEAD_DIM + 1])
    o_ref[0] = jnp.concatenate(
        [o[:, hq * tq:(hq + 1) * tq].T for hq in range(GROUP)], axis=1).astype(BF16)


def _bound_stats(qn, kn):
    b = qn.shape[0]
    kmax2 = jnp.max(kn, axis=(2, 3))
    qmax2 = jnp.max(qn.reshape(b, KV_HEADS, -1), axis=2)
    return jnp.stack([kmax2, qmax2 * kmax2], axis=-1).reshape(-1)


def _read_bound_stats(stats_ref):
    base = (pl.program_id(0) * KV_HEADS + pl.program_id(1)) * 2
    return stats_ref[base], stats_ref[base + 1] <= _MAX_FIXED_STABILISER ** 2


def _attn_a_kernel(stats_ref, q_ref, k_ref, v_ref, qn_ref, o_ref, *, tk, unroll):
    tq = q_ref.shape[3]
    n = GROUP * tq
    n_chunks = k_ref.shape[2] // tk
    q = jnp.concatenate([q_ref[0, hq] for hq in range(GROUP)], axis=1)
    kmax2, fixed_ok = _read_bound_stats(stats_ref)

    def chunk(c):
        start = pl.multiple_of(c * tk, tk)
        return k_ref[0, 0, pl.ds(start, tk), :], v_ref[0, 0, :, pl.ds(start, tk)]

    @pl.when(fixed_ok)
    def _():
        qn2 = jnp.concatenate([qn_ref[0, hq] for hq in range(GROUP)], axis=1)
        bound = jnp.sqrt(qn2 * kmax2)

        lt = V7X_MXU_COLS
        lanes = [slice(t * lt, (t + 1) * lt) for t in range(n // lt)]

        def scores(c, sl):
            return jnp.dot(chunk(c)[0], q[:, sl], preferred_element_type=F32)

        def consume(st, c, sl, acc):
            p = jnp.exp2(st - bound[:, sl]).astype(BF16)
            return acc + jnp.dot(chunk(c)[1], p, preferred_element_type=F32)

        def body(c, carry):
            sts, accs = carry
            new_sts, new_accs = [], []
            for t, sl in enumerate(lanes):
                new_sts.append(scores(c + 1, sl))
                new_accs.append(consume(sts[t], c, sl, accs[t]))
            return tuple(new_sts), tuple(new_accs)

        init = (tuple(scores(0, sl) for sl in lanes),
                tuple(jnp.zeros((v_ref.shape[2], lt), F32) for _ in lanes))
        sts, accs = lax.fori_loop(0, n_chunks - 1, body, init, unroll=unroll)
        acc = jnp.concatenate(
            [consume(sts[t], n_chunks - 1, sl, accs[t]) for t, sl in enumerate(lanes)], axis=1)
        _attn_a_finish(acc, o_ref, tq)

    @pl.when(jnp.logical_not(fixed_ok))
    def _():
        def body(c, carry):
            m, acc = carry
            kc, vc = chunk(c)
            st = jnp.dot(kc, q, preferred_element_type=F32)
            m_new = jnp.maximum(m, jnp.max(st, axis=0, keepdims=True))
            p = jnp.exp2(st - m_new).astype(BF16)
            acc = acc * jnp.exp2(m - m_new) + jnp.dot(vc, p, preferred_element_type=F32)
            return m_new, acc

        init = (jnp.full((1, n), NEG_INF, F32), jnp.zeros((v_ref.shape[2], n), F32))
        _, acc = lax.fori_loop(0, n_chunks, body, init)
        _attn_a_finish(acc, o_ref, tq)


def _attn_a_call(qt, k, vt, qn, kn, *, tq=ATTN_A_QUERIES, tk=ATTN_A_KEYS, unroll=True):
    b, _, _, s = qt.shape
    assert s % tq == 0 and s % tk == 0 and s // tk > 1 and (GROUP * tq) % V7X_MXU_COLS == 0
    return pl.pallas_call(
        functools.partial(_attn_a_kernel, tk=tk, unroll=unroll),
        out_shape=jax.ShapeDtypeStruct((b, s, Q_HEADS * HEAD_DIM), BF16),
        grid=(b, KV_HEADS, s // tq),
        in_specs=[
            pl.BlockSpec(memory_space=pltpu.SMEM),
            pl.BlockSpec((1, GROUP, HEAD_DIM, tq), lambda bi, hk, i: (bi, hk, 0, i)),
            pl.BlockSpec((1, 1, s, HEAD_DIM), lambda bi, hk, i: (bi, hk, 0, 0)),
            pl.BlockSpec((1, 1, vt.shape[2], s), lambda bi, hk, i: (bi, hk, 0, 0)),
            pl.BlockSpec((1, GROUP, 1, tq), lambda bi, hk, i: (bi, hk, 0, i)),
        ],
        out_specs=pl.BlockSpec((1, tq, GROUP * HEAD_DIM), lambda bi, hk, i: (bi, i, hk)),
        compiler_params=pltpu.CompilerParams(
            dimension_semantics=("arbitrary", "arbitrary", "arbitrary"),
            vmem_limit_bytes=V7X_VMEM_LIMIT),
        name="attn_global",
    )(_bound_stats(qn, kn), qt, k, vt, qn)


def _attn_b_kernel(sink_ref, stats_ref, bias_ref, q_ref, kl_ref, kc_ref, kr_ref,
                   vl_ref, vc_ref, vr_ref, qn_ref, o_ref, *, nsub):
    hk = pl.program_id(1)
    i = pl.program_id(2)
    last_i = pl.num_programs(2) - 1
    w = kl_ref.shape[2]
    k = jnp.concatenate([kl_ref[0, 0], kc_ref[0, 0], kr_ref[0, 0]], axis=0)
    v = jnp.concatenate([vl_ref[0, 0], vc_ref[0, 0], vr_ref[0, 0]], axis=1)
    sink = jnp.concatenate(
        [jnp.full((1, w), sink_ref[hk * GROUP + hq] * LOG2E, F32) for hq in range(GROUP)],
        axis=1)
    band = bias_ref[1]
    kmax2, fixed_ok = _read_bound_stats(stats_ref)

    def block_lanes(ref, j):
        return jnp.concatenate([ref[0, hq, :, j * w:(j + 1) * w] for hq in range(GROUP)], axis=1)

    def masked_scores(j):
        bias = band
        if j == 0:
            bias = jnp.where(i == 0, bias_ref[0], band)
        if j == nsub - 1:
            bias = jnp.where(i == last_i, bias_ref[2], bias)
        st = jnp.dot(k[j * w:(j + 3) * w], block_lanes(q_ref, j),
                     preferred_element_type=F32)
        return st + jnp.concatenate([bias] * GROUP, axis=1)

    def finish(j, st, m):
        p = jnp.exp2(st - m).astype(BF16)
        acc = jnp.dot(v[:, j * w:(j + 3) * w], p, preferred_element_type=F32)
        denom = acc[HEAD_DIM:HEAD_DIM + 1] + jnp.exp2(sink - m)
        o = acc[:HEAD_DIM] * (1.0 / denom)
        o_ref[0, j * w:(j + 1) * w, :] = jnp.concatenate(
            [o[:, hq * w:(hq + 1) * w].T for hq in range(GROUP)], axis=1).astype(BF16)

    @pl.when(fixed_ok)
    def _():
        ahead = 2
        sts = [masked_scores(j) for j in range(min(ahead, nsub))]
        for j in range(nsub):
            if j + ahead < nsub:
                sts.append(masked_scores(j + ahead))
            bound = jnp.sqrt(block_lanes(qn_ref, j) * kmax2)
            finish(j, sts[j], jnp.maximum(bound, sink))

    @pl.when(jnp.logical_not(fixed_ok))
    def _():
        for j in range(nsub):
            st = masked_scores(j)
            finish(j, st, jnp.maximum(jnp.max(st, axis=0, keepdims=True), sink))


def _band_bias():
    w = WINDOW
    r = np.arange(3 * w)[:, None] - w
    c = np.arange(w)[None, :]
    band = np.abs(r - c) <= w
    first = band & (r >= 0)
    last = band & (r < w)
    table = np.stack([first, band, last]).astype(np.float32)
    return (1.0 - table) * NEG_INF


def _attn_b_call(sink, qt, k, vt, qn, kn, *, nsub=ATTN_B_BLOCKS):
    b, _, _, s = qt.shape
    assert s % (nsub * WINDOW) == 0
    w = WINDOW
    nb = s // w
    span = nsub * w
    bias = jnp.asarray(_band_bias(), F32)
    left = lambda i: jnp.maximum(i * nsub - 1, 0)
    right = lambda i: jnp.minimum((i + 1) * nsub, nb - 1)

    return pl.pallas_call(
        functools.partial(_attn_b_kernel, nsub=nsub),
        out_shape=jax.ShapeDtypeStruct((b, s, Q_HEADS * HEAD_DIM), BF16),
        grid=(b, KV_HEADS, s // span),
        in_specs=[
            pl.BlockSpec(memory_space=pltpu.SMEM),
            pl.BlockSpec(memory_space=pltpu.SMEM),
            pl.BlockSpec((3, 3 * w, w), lambda bi, hk, i: (0, 0, 0)),
            pl.BlockSpec((1, GROUP, HEAD_DIM, span), lambda bi, hk, i: (bi, hk, 0, i)),
            pl.BlockSpec((1, 1, w, HEAD_DIM), lambda bi, hk, i: (bi, hk, left(i), 0)),
            pl.BlockSpec((1, 1, span, HEAD_DIM), lambda bi, hk, i: (bi, hk, i, 0)),
            pl.BlockSpec((1, 1, w, HEAD_DIM), lambda bi, hk, i: (bi, hk, right(i), 0)),
            pl.BlockSpec((1, 1, vt.shape[2], w), lambda bi, hk, i: (bi, hk, 0, left(i))),
            pl.BlockSpec((1, 1, vt.shape[2], span), lambda bi, hk, i: (bi, hk, 0, i)),
            pl.BlockSpec((1, 1, vt.shape[2], w), lambda bi, hk, i: (bi, hk, 0, right(i))),
            pl.BlockSpec((1, GROUP, 1, span), lambda bi, hk, i: (bi, hk, 0, i)),
        ],
        out_specs=pl.BlockSpec((1, span, GROUP * HEAD_DIM), lambda bi, hk, i: (bi, i, hk)),
        compiler_params=pltpu.CompilerParams(
            dimension_semantics=("arbitrary", "arbitrary", "arbitrary"),
            vmem_limit_bytes=V7X_VMEM_LIMIT),
        name="attn_window",
    )(sink, _bound_stats(qn, kn), bias, qt, k, k, k, vt, vt, vt, qn)


def _channel_kernel(x_ref, mod_ref, ya_ref, yb_ref, g1_ref, wg_ref, wb_ref, wo_ref,
                    g2_ref, gf_ref, w1_ref, w2_ref, o_ref, *, n_sub, n_chunks):
    d = x_ref.shape[2]
    ts = x_ref.shape[1] // n_sub
    cw = w1_ref.shape[1] // n_chunks
    dot = functools.partial(jnp.dot, preferred_element_type=F32)
    gate1 = mod_ref[0, 2:3, :]
    shift2 = mod_ref[0, 3:4, :]
    scale2 = mod_ref[0, 4:5, :]
    gate2 = mod_ref[0, 5:6, :]
    shift1 = mod_ref[0, 0:1, :]
    scale1 = mod_ref[0, 1:2, :]
    n_qkv = wg_ref.shape[1] - 2 * d

    def sigmoid(g):
        return 0.5 * jnp.tanh(0.5 * g) + 0.5

    for sub in range(n_sub):
        tok = slice(sub * ts, (sub + 1) * ts)
        x = x_ref[0, tok, :]
        h = (_rms(x) * g1_ref[...] * (1.0 + scale1) + shift1).astype(BF16)
        merged = (sigmoid(dot(h, wg_ref[:, n_qkv:n_qkv + d])) * dot(ya_ref[0, tok, :], wb_ref[0])
                  + sigmoid(dot(h, wg_ref[:, n_qkv + d:])) * dot(yb_ref[0, tok, :], wb_ref[1]))
        x1 = x + gate1 * dot(merged.astype(BF16), wo_ref[...])

        h2 = (_rms(x1) * g2_ref[...] * (1.0 + scale2) + shift2).astype(BF16)
        acc = jnp.zeros((ts, d), F32)
        for c in range(n_chunks):
            hid = dot(h2, w1_ref[:, c * cw:(c + 1) * cw])
            hid = jnp.square(jnp.maximum(hid, 0.0)).astype(BF16)
            acc = acc + dot(hid, w2_ref[c * cw:(c + 1) * cw, :])
        x2 = x1 + gate2 * acc
        o_ref[0, tok, :] = _rms(x2) * gf_ref[...]


def _channel_call(x, mod3, ya, yb, g1, w_in, wb, wo, g2, gf, w1, w2, *,
                  tm=CHANNEL_ROWS, n_sub=CHANNEL_SUBTILES, n_chunks=MLP_CHUNKS):
    b, s, d = x.shape
    assert s % tm == 0 and tm % n_sub == 0 and w1.shape[1] % n_chunks == 0
    bw = ya.shape[2]
    ff = w1.shape[1]
    tok = lambda bi, i: (bi, i, 0)
    const2 = lambda bi, i: (0, 0)
    once = pl.Buffered(1)
    return pl.pallas_call(
        functools.partial(_channel_kernel, n_sub=n_sub, n_chunks=n_chunks),
        out_shape=jax.ShapeDtypeStruct((b, s, d), F32),
        grid=(b, s // tm),
        in_specs=[
            pl.BlockSpec((1, tm, d), tok),
            pl.BlockSpec((1, 6, d), lambda bi, i: (bi, 0, 0)),
            pl.BlockSpec((1, tm, bw), tok),
            pl.BlockSpec((1, tm, bw), tok),
            pl.BlockSpec((1, d), const2),
            pl.BlockSpec(w_in.shape, const2, pipeline_mode=once),
            pl.BlockSpec((2, bw, d), lambda bi, i: (0, 0, 0), pipeline_mode=once),
            pl.BlockSpec((d, d), const2, pipeline_mode=once),
            pl.BlockSpec((1, d), const2),
            pl.BlockSpec((1, d), const2),
            pl.BlockSpec((d, ff), const2, pipeline_mode=once),
            pl.BlockSpec((ff, d), const2, pipeline_mode=once),
        ],
        out_specs=pl.BlockSpec((1, tm, d), tok),
        compiler_params=pltpu.CompilerParams(
            dimension_semantics=("arbitrary", "arbitrary"),
            vmem_limit_bytes=V7X_VMEM_LIMIT),
        name="merge_mlp_final_norm",
    )(x, mod3, ya, yb, g1, w_in, wb, wo, g2, gf, w1, w2)


def _rope_tables(s):
    def cos_sin(pos, dim, theta):
        inv = theta ** (-np.arange(0, dim, 2, dtype=np.float64) / dim)
        ang = pos.astype(np.float64)[:, None] * inv[None, :]
        return np.cos(ang).T, np.sin(ang).T

    t = np.arange(s)
    cr, sr = cos_sin(t // GRID_W, HEAD_DIM // 2, AXIAL_THETA)
    cc, sc = cos_sin(t % GRID_W, HEAD_DIM // 2, AXIAL_THETA)
    c1, s1 = cos_sin(t, HEAD_DIM, ROPE_THETA)
    tables = (np.concatenate([cr, cr, cc, cc]), np.concatenate([-sr, sr, -sc, sc]),
              np.concatenate([c1, c1]), np.concatenate([-s1, s1]))
    return tuple(jnp.asarray(tab.astype(np.float32)) for tab in tables)


def kernel(x, c, w_ada, b_ada, norm1_g, w_in, q_norm_a, k_norm_a, sink_b, w_branch, w_out,
           norm2_g, w_mlp_in, w_mlp_out, final_g):
    b, s, d = x.shape
    assert w_ada.shape[0] == 1, "kernel supports DEPTH == 1"
    l = 0
    cos_a, sin_a, cos_b, sin_b = _rope_tables(s)

    mod3 = _mod_call(c, w_ada[l], b_ada[l][None, :]).reshape(b, 6, d)
    w_in_bf = w_in[l].astype(BF16)
    g1 = norm1_g[l][None, :]
    qa, ka, va, qna, kna, qb, kb, vb, qnb, knb = _proj_call(
        x, mod3, g1, w_in_bf,
        q_norm_a[l][:, None], k_norm_a[l][:, None], cos_a, sin_a, cos_b, sin_b)
    ya = _attn_a_call(qa, ka, va, qna, kna)
    yb = _attn_b_call(sink_b[l], qb, kb, vb, qnb, knb)
    return _channel_call(x, mod3, ya, yb, g1, w_in_bf,
                         w_branch[l].astype(BF16), w_out[l].astype(BF16),
                         norm2_g[l][None, :], final_g[None, :],
                         w_mlp_in[l].astype(BF16), w_mlp_out[l].astype(BF16))
```
